```python
import math
import jax, jax.numpy as jnp
from jax import lax
import numpy as np

D_MODEL = 4096
BATCH = 1
SEQ = 16384
DEPTH = 4

N_MIXERS = 2
FFN_DIM = 3 * D_MODEL // 2
NORM_EPS = 1e-6
SSM_GROUP = 16
SSM_GROUPS = D_MODEL // SSM_GROUP
SSM_STATE = 64
SSM_CHUNK = 128
SSM_DT_MIN = 1e-3
SSM_DT_MAX = 1e-1
HEAD_DIM = 128
N_HEADS = D_MODEL // HEAD_DIM
N_KV_HEADS = 4
HEADS_PER_KV = N_HEADS // N_KV_HEADS
CMP_BLOCK = 32
CMP_STRIDE = 16
CMP_HIDDEN = 256
SEL_BLOCK = 64
N_SEL = 16
WINDOW = 512
Q_BLOCK = 128
N_BRANCH = 3
Q_WIDTH = N_HEADS * HEAD_DIM
KV_WIDTH = N_KV_HEADS * HEAD_DIM
NSA_IN = Q_WIDTH + 6 * KV_WIDTH + N_BRANCH * N_HEADS
NEG_INF = -1e30
FORCE_SCORE = 1e4

kernel_name = 'hybrid_s5_nsa_macaron'


def rms_norm(x, g):
    xf = x.astype(jnp.float32)
    y = xf * lax.rsqrt(jnp.mean(xf * xf, axis=-1, keepdims=True) + NORM_EPS)
    return (y * g.astype(jnp.float32)).astype(x.dtype)


def swiglu(h, w_in, w_out):
    a, b = jnp.split(h @ w_in, 2, axis=-1)
    return (jax.nn.silu(a) * b) @ w_out


def cmul(ar, ai, br, bi):
    return ar * br - ai * bi, ar * bi + ai * br


def masked_softmax(s, mask):
    p = jax.nn.softmax(jnp.where(mask, s, NEG_INF), axis=-1)
    return p * mask


def s5_mixer(h, lam_re, lam_im, log_step, b_re, b_im, c_re, c_im, d_skip, w_glu):
    bsz, L, dm = h.shape
    f32 = jnp.float32
    u = h.astype(f32)
    dt = jnp.exp(log_step.astype(f32))[:, None]
    lr, li = lam_re.astype(f32), lam_im.astype(f32)
    mag = jnp.exp(lr * dt)
    ang = li * dt
    ab_re, ab_im = mag * jnp.cos(ang), mag * jnp.sin(ang)
    den = lr * lr + li * li
    nr, ni = ab_re - 1.0, ab_im
    coef_re = (nr * lr + ni * li) / den
    coef_im = (ni * lr - nr * li) / den
    br, bi = b_re.astype(f32), b_im.astype(f32)
    bb_re = coef_re[..., None] * br - coef_im[..., None] * bi
    bb_im = coef_re[..., None] * bi + coef_im[..., None] * br
    cr_w, ci_w = c_re.astype(f32), c_im.astype(f32)
    n_chunks = L // SSM_CHUNK
    u_chunks = u.reshape(bsz, n_chunks, SSM_CHUNK, SSM_GROUPS, SSM_GROUP).transpose(1, 2, 0, 3, 4)
    a_re = jnp.broadcast_to(ab_re, (SSM_CHUNK, bsz, SSM_GROUPS, SSM_STATE))
    a_im = jnp.broadcast_to(ab_im, (SSM_CHUNK, bsz, SSM_GROUPS, SSM_STATE))

    def combine(e1, e2):
        a1r, a1i, b1r, b1i = e1
        a2r, a2i, b2r, b2i = e2
        ar, ai = cmul(a2r, a2i, a1r, a1i)
        tr, ti = cmul(a2r, a2i, b1r, b1i)
        return ar, ai, tr + b2r, ti + b2i

    def chunk_step(carry, u_c):
        hr, hi = carry
        bur = jnp.einsum('lbgh,gph->lbgp', u_c, bb_re)
        bui = jnp.einsum('lbgh,gph->lbgp', u_c, bb_im)
        pr_cum, pi_cum, sr, si = lax.associative_scan(combine, (a_re, a_im, bur, bui), axis=0)
        qr, qi = cmul(pr_cum, pi_cum, hr[None], hi[None])
        sr = sr + qr
        si = si + qi
        y = jnp.einsum('lbgp,ghp->lbgh', sr, cr_w) - jnp.einsum('lbgp,ghp->lbgh', si, ci_w)
        return (sr[-1], si[-1]), y

    h0 = jnp.zeros((bsz, SSM_GROUPS, SSM_STATE), f32)
    _, ys = lax.scan(chunk_step, (h0, h0), u_chunks)
    y = ys.transpose(2, 0, 1, 3, 4).reshape(bsz, L, dm) + d_skip.astype(f32) * u
    g = jax.nn.gelu(y).astype(h.dtype)
    val, gate = jnp.split(g @ w_glu, 2, axis=-1)
    return val * jax.nn.sigmoid(gate)


def compress(k, pe, w1, w2):
    bsz, L, g, dk = k.shape
    n_cmp = (L - CMP_BLOCK) // CMP_STRIDE + 1
    idx = jnp.arange(n_cmp)[:, None] * CMP_STRIDE + jnp.arange(CMP_BLOCK)[None, :]
    blocks = k[:, idx] + pe[None, None, :, None, :]
    flat = blocks.transpose(0, 1, 3, 2, 4).reshape(bsz, n_cmp, g, CMP_BLOCK * dk)
    return jax.nn.gelu(flat @ w1) @ w2


def nsa_mixer(h, w_in, w_o, pe_k, w1_k, w2_k, pe_v, w1_v, w2_v):
    bsz, L, _ = h.shape
    f32 = jnp.float32
    proj = (h @ w_in).astype(f32)
    splits = list(np.cumsum([Q_WIDTH] + [KV_WIDTH] * 6))
    parts = jnp.split(proj, splits, axis=-1)
    q = parts[0].reshape(bsz, L, N_KV_HEADS, HEADS_PER_KV, HEAD_DIM) * (HEAD_DIM ** -0.5)
    kc_raw, vc_raw, ks_raw, vs_raw, kw_raw, vw_raw = [p.reshape(bsz, L, N_KV_HEADS, HEAD_DIM) for p in parts[1:7]]
    gates = jax.nn.sigmoid(parts[7]).reshape(bsz, L, N_KV_HEADS, HEADS_PER_KV, N_BRANCH)
    k_cmp = compress(kc_raw, pe_k.astype(f32), w1_k.astype(f32), w2_k.astype(f32))
    v_cmp = compress(vc_raw, pe_v.astype(f32), w1_v.astype(f32), w2_v.astype(f32))
    n_cmp = k_cmp.shape[1]
    n_sb = L // SEL_BLOCK
    n_pick = min(N_SEL, n_sb)
    k_sel = ks_raw.reshape(bsz, n_sb, SEL_BLOCK, N_KV_HEADS, HEAD_DIM).transpose(0, 3, 1, 2, 4)
    v_sel = vs_raw.reshape(bsz, n_sb, SEL_BLOCK, N_KV_HEADS, HEAD_DIM).transpose(0, 3, 1, 2, 4)
    pad = ((0, 0), (WINDOW, 0), (0, 0), (0, 0))
    k_win = jnp.pad(kw_raw, pad)
    v_win = jnp.pad(vw_raw, pad)
    cmp_start = jnp.arange(n_cmp) * CMP_STRIDE
    cmp_end = cmp_start + CMP_BLOCK - 1
    sel_start = jnp.arange(n_sb) * SEL_BLOCK
    overlap = ((cmp_start[:, None] < sel_start[None, :] + SEL_BLOCK)
               & (cmp_start[:, None] + CMP_BLOCK > sel_start[None, :])).astype(f32)
    b_ix = jnp.arange(bsz)[:, None, None, None]
    g_ix = jnp.arange(N_KV_HEADS)[None, None, :, None]
    i_q = jnp.arange(Q_BLOCK)[:, None]
    j_k = jnp.arange(WINDOW + Q_BLOCK)[None, :]
    j_blk = jnp.arange(n_sb)[None, :]

    def query_block(qb_i):
        start = qb_i * Q_BLOCK
        t = start + jnp.arange(Q_BLOCK)
        qb = lax.dynamic_slice_in_dim(q, start, Q_BLOCK, axis=1)
        gb = lax.dynamic_slice_in_dim(gates, start, Q_BLOCK, axis=1)
        m_c = (cmp_end[None, :] <= t[:, None])[None, :, None, None, :]
        p_c = masked_softmax(jnp.einsum('bqgnd,bcgd->bqgnc', qb, k_cmp), m_c)
        o_c = jnp.einsum('bqgnc,bcgd->bqgnd', p_c, v_cmp)
        imp = jnp.einsum('bqgnc,cj->bqgj', p_c, overlap)
        blk = (t // SEL_BLOCK)[:, None]
        forced = (j_blk == 0) | (j_blk == blk) | (j_blk == blk - 1)
        causal = j_blk <= blk
        score = jnp.where(forced[None, :, None, :], FORCE_SCORE,
                          jnp.where(causal[None, :, None, :], imp, -1.0))
        top_val, top_idx = lax.top_k(score, n_pick)
        kg = k_sel[b_ix, g_ix, top_idx].reshape(bsz, Q_BLOCK, N_KV_HEADS, n_pick * SEL_BLOCK, HEAD_DIM)
        vg = v_sel[b_ix, g_ix, top_idx].reshape(bsz, Q_BLOCK, N_KV_HEADS, n_pick * SEL_BLOCK, HEAD_DIM)
        pos = top_idx[..., None] * SEL_BLOCK + jnp.arange(SEL_BLOCK)
        m_s = ((top_val >= 0.0)[..., None] & (pos <= t[None, :, None, None, None]))
        m_s = m_s.reshape(bsz, Q_BLOCK, N_KV_HEADS, n_pick * SEL_BLOCK)[:, :, :, None, :]
        p_s = masked_softmax(jnp.einsum('bqgnd,bqgkd->bqgnk', qb, kg), m_s)
        o_s = jnp.einsum('bqgnk,bqgkd->bqgnd', p_s, vg)
        kw = lax.dynamic_slice_in_dim(k_win, start, WINDOW + Q_BLOCK, axis=1)
        vw = lax.dynamic_slice_in_dim(v_win, start, WINDOW + Q_BLOCK, axis=1)
        m_w = ((j_k > i_q) & (j_k <= i_q + WINDOW) & (j_k + start >= WINDOW))[None, :, None, None, :]
        p_w = masked_softmax(jnp.einsum('bqgnd,bkgd->bqgnk', qb, kw), m_w)
        o_w = jnp.einsum('bqgnk,bkgd->bqgnd', p_w, vw)
        return gb[..., 0:1] * o_c + gb[..., 1:2] * o_s + gb[..., 2:3] * o_w

    o = lax.map(query_block, jnp.arange(L // Q_BLOCK))
    o = o.transpose(1, 0, 2, 3, 4, 5).reshape(bsz, L, Q_WIDTH).astype(h.dtype)
    return o @ w_o


def setup_inputs(seed: int = 0) -> dict:
    key = jax.random.key(seed)
    ks = iter(jax.random.split(key, 24))
    f32 = jnp.float32
    n_ssm = (DEPTH + 1) // 2
    n_nsa = DEPTH // 2

    def normal(shape, scale):
        return jax.random.normal(next(ks), shape, f32) * scale

    G, P, H = SSM_GROUPS, SSM_STATE, SSM_GROUP
    x = normal((BATCH, SEQ, D_MODEL), 1.0)
    norm_g = 1.0 + normal((DEPTH, 3, D_MODEL), 0.02)
    final_norm_g = 1.0 + normal((D_MODEL,), 0.02)
    ffn_w_in = normal((DEPTH, 2, D_MODEL, 2 * FFN_DIM), D_MODEL ** -0.5)
    ffn_w_out = normal((DEPTH, 2, FFN_DIM, D_MODEL), FFN_DIM ** -0.5)
    s5_lambda_re = -0.5 + jax.random.uniform(next(ks), (n_ssm, G, P), f32, -0.02, 0.02)
    s5_lambda_im = math.pi * jnp.arange(P, dtype=f32) + normal((n_ssm, G, P), 0.01)
    s5_log_step = jax.random.uniform(next(ks), (n_ssm, G), f32, math.log(SSM_DT_MIN), math.log(SSM_DT_MAX))
    s5_b_re = normal((n_ssm, G, P, H), (2 * H) ** -0.5)
    s5_b_im = normal((n_ssm, G, P, H), (2 * H) ** -0.5)
    s5_c_re = normal((n_ssm, G, H, P), 0.5)
    s5_c_im = normal((n_ssm, G, H, P), 0.5)
    s5_d = normal((n_ssm, D_MODEL), 1.0)
    s5_w_glu = normal((n_ssm, D_MODEL, 2 * D_MODEL), D_MODEL ** -0.5)
    nsa_w_in = normal((n_nsa, D_MODEL, NSA_IN), D_MODEL ** -0.5)
    nsa_w_o = normal((n_nsa, Q_WIDTH, D_MODEL), Q_WIDTH ** -0.5)
    nsa_pe_k = normal((n_nsa, CMP_BLOCK, HEAD_DIM), 0.1)
    nsa_w1_k = normal((n_nsa, CMP_BLOCK * HEAD_DIM, CMP_HIDDEN), (CMP_BLOCK * HEAD_DIM) ** -0.5)
    nsa_w2_k = normal((n_nsa, CMP_HIDDEN, HEAD_DIM), CMP_HIDDEN ** -0.5)
    nsa_pe_v = normal((n_nsa, CMP_BLOCK, HEAD_DIM), 0.1)
    nsa_w1_v = normal((n_nsa, CMP_BLOCK * HEAD_DIM, CMP_HIDDEN), (CMP_BLOCK * HEAD_DIM) ** -0.5)
    nsa_w2_v = normal((n_nsa, CMP_HIDDEN, HEAD_DIM), CMP_HIDDEN ** -0.5)
    return {'x': x, 'norm_g': norm_g, 'final_norm_g': final_norm_g,
            'ffn_w_in': ffn_w_in, 'ffn_w_out': ffn_w_out,
            's5_lambda_re': s5_lambda_re, 's5_lambda_im': s5_lambda_im, 's5_log_step': s5_log_step,
            's5_b_re': s5_b_re, 's5_b_im': s5_b_im, 's5_c_re': s5_c_re, 's5_c_im': s5_c_im,
            's5_d': s5_d, 's5_w_glu': s5_w_glu,
            'nsa_w_in': nsa_w_in, 'nsa_w_o': nsa_w_o,
            'nsa_pe_k': nsa_pe_k, 'nsa_w1_k': nsa_w1_k, 'nsa_w2_k': nsa_w2_k,
            'nsa_pe_v': nsa_pe_v, 'nsa_w1_v': nsa_w1_v, 'nsa_w2_v': nsa_w2_v}


def reference(x, norm_g, final_norm_g, ffn_w_in, ffn_w_out,
              s5_lambda_re, s5_lambda_im, s5_log_step, s5_b_re, s5_b_im, s5_c_re, s5_c_im,
              s5_d, s5_w_glu,
              nsa_w_in, nsa_w_o, nsa_pe_k, nsa_w1_k, nsa_w2_k, nsa_pe_v, nsa_w1_v, nsa_w2_v):
    for layer in range(DEPTH):
        x = x + 0.5 * swiglu(rms_norm(x, norm_g[layer, 0]), ffn_w_in[layer, 0], ffn_w_out[layer, 0])
        h = rms_norm(x, norm_g[layer, 1])
        j = layer // N_MIXERS
        if layer % N_MIXERS == 0:
            x = x + s5_mixer(h, s5_lambda_re[j], s5_lambda_im[j], s5_log_step[j],
                             s5_b_re[j], s5_b_im[j], s5_c_re[j], s5_c_im[j], s5_d[j], s5_w_glu[j])
        else:
            x = x + nsa_mixer(h, nsa_w_in[j], nsa_w_o[j], nsa_pe_k[j], nsa_w1_k[j], nsa_w2_k[j],
                              nsa_pe_v[j], nsa_w1_v[j], nsa_w2_v[j])
        x = x + 0.5 * swiglu(rms_norm(x, norm_g[layer, 2]), ffn_w_in[layer, 1], ffn_w_out[layer, 1])
    return rms_norm(x, final_norm_g)
```

```python
import functools
import math

import jax
import jax.numpy as jnp
from jax import lax
from jax.experimental import pallas as pl
from jax.experimental.pallas import tpu as pltpu

F32 = jnp.float32
BF16 = jnp.bfloat16

D_MODEL = 4096
DEPTH = 4
FFN_DIM = 3 * D_MODEL // 2
NORM_EPS = 1e-6
SSM_GROUP = 16
SSM_GROUPS = D_MODEL // SSM_GROUP
SSM_STATE = 64
HEAD_DIM = 128
N_HEADS = D_MODEL // HEAD_DIM
N_KV_HEADS = 4
HEADS_PER_KV = N_HEADS // N_KV_HEADS
CMP_BLOCK = 32
CMP_STRIDE = 16
CMP_HIDDEN = 256
SEL_BLOCK = 64
N_SEL = 16
WINDOW = 512
N_BRANCH = 3
Q_WIDTH = N_HEADS * HEAD_DIM
KV_WIDTH = N_KV_HEADS * HEAD_DIM
NEG_INF = -1e30
FORCE_SCORE = 1e4

V7X_SUBLANES = 8
V7X_LANES = 128
V7X_VMEM_BYTES = 64 * 1024 * 1024
VMEM_LIMIT = V7X_VMEM_BYTES - 6 * 1024 * 1024

SCAN_SEGS = V7X_SUBLANES
SCAN_SUPER = 8
SCAN_COLS = SCAN_SUPER * SSM_STATE
SCAN_CH = SCAN_SUPER * SSM_GROUP
N_SUPER = SSM_GROUPS // SCAN_SUPER

Q_TILE = 128
Q_ROWS = Q_TILE * HEADS_PER_KV
SEL_TILE = 512
BIAS_BLOCKS = 128
BIAS_KEYS = BIAS_BLOCKS * SEL_BLOCK


def _params(*sem):
    return pltpu.CompilerParams(dimension_semantics=sem, vmem_limit_bytes=VMEM_LIMIT)


def _gelu(x):
    c = math.sqrt(2.0 / math.pi)
    return 0.5 * x * (1.0 + jnp.tanh(c * (x + 0.044715 * (x * x * x))))


def _dot(a, b):
    return jnp.dot(a, b, preferred_element_type=F32)


def _dot_nt(a, b):
    return lax.dot_general(a, b, (((1,), (1,)), ((), ())), preferred_element_type=F32)


def _rmsnorm_kernel(x_ref, g_ref, o_ref):
    x = x_ref[...]
    ms = jnp.mean(x * x, axis=-1, keepdims=True)
    o_ref[...] = ((x * lax.rsqrt(ms + NORM_EPS)) * g_ref[...]).astype(o_ref.dtype)


def rmsnorm(x, g, out_dtype, tm=256):
    m, d = x.shape
    return pl.pallas_call(
        _rmsnorm_kernel,
        grid=(m // tm,),
        in_specs=[pl.BlockSpec((tm, d), lambda i: (i, 0)),
                  pl.BlockSpec((1, d), lambda i: (0, 0))],
        out_specs=pl.BlockSpec((tm, d), lambda i: (i, 0)),
        out_shape=jax.ShapeDtypeStruct((m, d), out_dtype),
        compiler_params=_params("parallel"),
        name="rmsnorm",
    )(x, g.reshape(1, d).astype(F32))


def _mm_kernel(*refs, n_rhs, has_res, epilogue):
    lhs = refs[0][...]
    accs = [_dot(lhs, r[...]) for r in refs[1:1 + n_rhs]]
    res = refs[1 + n_rhs][...] if has_res else None
    o_ref = refs[-1]
    o_ref[...] = epilogue(accs, res).astype(o_ref.dtype)


def matmul(lhs, rhs, rhs_col_offsets, n_out, epilogue, out_dtype, res=None, tm=1024, tn=512, name="matmul"):
    m, k = lhs.shape
    tn = min(tn, n_out)
    tm = min(tm, m)
    n_rhs = len(rhs_col_offsets)
    in_specs = [pl.BlockSpec((tm, k), lambda i, j: (i, 0))]
    for off in rhs_col_offsets:
        in_specs.append(pl.BlockSpec((k, tn), functools.partial(lambda i, j, ob: (0, j + ob), ob=off // tn)))
    args = [lhs] + [rhs] * n_rhs
    if res is not None:
        in_specs.append(pl.BlockSpec((tm, tn), lambda i, j: (i, j)))
        args.append(res)
    return pl.pallas_call(
        functools.partial(_mm_kernel, n_rhs=n_rhs, has_res=res is not None, epilogue=epilogue),
        grid=(m // tm, n_out // tn),
        in_specs=in_specs,
        out_specs=pl.BlockSpec((tm, tn), lambda i, j: (i, j)),
        out_shape=jax.ShapeDtypeStruct((m, n_out), out_dtype),
        compiler_params=_params("parallel", "arbitrary"),
        name=name,
    )(*args)


def _ep_swiglu(accs, res):
    a, b = accs
    return (a * jax.nn.sigmoid(a)) * b


def _ep_half_residual(accs, res):
    return res + 0.5 * accs[0]


def _ep_residual(accs, res):
    return res + accs[0]


def _ep_glu_residual(accs, res):
    val, gate = accs
    return res + val * jax.nn.sigmoid(gate)


def _ep_scale_q(accs, res):
    return accs[0] * (HEAD_DIM ** -0.5)


def _ep_identity(accs, res):
    return accs[0]


def _ep_sigmoid(accs, res):
    return jax.nn.sigmoid(accs[0])


def ffn(x, g, w_in, w_out):
    h = rmsnorm(x, g, BF16)
    act = matmul(h, w_in, (0, FFN_DIM), FFN_DIM, _ep_swiglu, BF16, name="ffn_in")
    return matmul(act, w_out, (0,), D_MODEL, _ep_half_residual, F32, res=x, name="ffn_out")


def _s5_discretise(lam_re, lam_im, log_step, b_re, b_im, seg_len):
    dt = jnp.exp(log_step.astype(F32))[:, None]
    lr, li = lam_re.astype(F32), lam_im.astype(F32)
    mag = jnp.exp(lr * dt)
    ang = li * dt
    ab_re, ab_im = mag * jnp.cos(ang), mag * jnp.sin(ang)
    den = lr * lr + li * li
    nr, ni = ab_re - 1.0, ab_im
    coef_re = (nr * lr + ni * li) / den
    coef_im = (ni * lr - nr * li) / den
    br, bi = b_re.astype(F32), b_im.astype(F32)
    bb_re = coef_re[..., None] * br - coef_im[..., None] * bi
    bb_im = coef_re[..., None] * bi + coef_im[..., None] * br
    mag_n = jnp.exp(seg_len * (lr * dt))
    ang_n = seg_len * ang
    return ab_re, ab_im, bb_re, bb_im, mag_n * jnp.cos(ang_n), mag_n * jnp.sin(ang_n)


def _s5_layout(ab_re, ab_im, bb_re, bb_im, c_re, c_im, an_re, an_im):
    eye = jnp.eye(SCAN_SUPER, dtype=F32)

    def in_proj(bb):
        w = bb.reshape(N_SUPER, SCAN_SUPER, SSM_STATE, SSM_GROUP).transpose(0, 1, 3, 2)
        w = jnp.einsum('sghp,gk->sghkp', w, eye)
        return w.reshape(N_SUPER, SCAN_CH, SCAN_COLS)

    def out_proj(c):
        w = c.reshape(N_SUPER, SCAN_SUPER, SSM_GROUP, SSM_STATE).transpose(0, 1, 3, 2)
        w = jnp.einsum('sgph,gk->sgpkh', w, eye)
        return w.reshape(N_SUPER, SCAN_COLS, SCAN_CH)

    w_b = jnp.concatenate([in_proj(bb_re), in_proj(bb_im)], axis=2).astype(BF16)
    w_c = jnp.concatenate([out_proj(c_re.astype(F32)), -out_proj(c_im.astype(F32))], axis=1).astype(BF16)

    def lanes(a):
        a = a.reshape(N_SUPER, 1, SCAN_COLS)
        return jnp.broadcast_to(a, (N_SUPER, SCAN_SEGS, SCAN_COLS))

    return w_b, w_c, lanes(ab_re), lanes(ab_im), an_re.reshape(N_SUPER, SCAN_COLS), an_im.reshape(N_SUPER, SCAN_COLS)


def _s5_scan_steps(bu_scr, h_scr, a_re, a_im, h_re, h_im, tb):
    def step(t, carry):
        hr, hi = carry
        r0 = pl.multiple_of(t * SCAN_SEGS, SCAN_SEGS)
        bu = bu_scr[pl.ds(r0, SCAN_SEGS), :]
        nr = (a_re * hr - a_im * hi) + bu[:, :SCAN_COLS]
        ni = (a_re * hi + a_im * hr) + bu[:, SCAN_COLS:]
        if h_scr is not None:
            h_scr[pl.ds(r0, SCAN_SEGS), :SCAN_COLS] = nr
            h_scr[pl.ds(r0, SCAN_SEGS), SCAN_COLS:] = ni
        return nr, ni

    return lax.fori_loop(0, tb, step, (h_re, h_im), unroll=8)


def _s5_state_kernel(u_ref, wb_ref, are_ref, aim_ref, hend_ref, bu_scr, st_scr, *, tb):
    tblk = pl.program_id(1)

    @pl.when(tblk == 0)
    def _():
        st_scr[...] = jnp.zeros_like(st_scr)

    bu_scr[...] = _dot(u_ref[...].astype(BF16), wb_ref[0])
    st = st_scr[...]
    hr, hi = _s5_scan_steps(bu_scr, None, are_ref[0], aim_ref[0], st[:, :SCAN_COLS], st[:, SCAN_COLS:], tb)
    st_scr[:, :SCAN_COLS] = hr
    st_scr[:, SCAN_COLS:] = hi

    @pl.when(tblk == pl.num_programs(1) - 1)
    def _():
        hend_ref[0] = st_scr[...]


def _s5_carry_kernel(hend_ref, an_re_ref, an_im_ref, s_ref):
    ar, ai = an_re_ref[...], an_im_ref[...]
    sr = jnp.zeros_like(ar)
    si = jnp.zeros_like(ar)
    for c in range(SCAN_SEGS):
        s_ref[c, :, :SCAN_COLS] = sr
        s_ref[c, :, SCAN_COLS:] = si
        hr = hend_ref[c, :, :SCAN_COLS]
        hi = hend_ref[c, :, SCAN_COLS:]
        sr, si = (ar * sr - ai * si) + hr, (ar * si + ai * sr) + hi


def _s5_output_kernel(u_ref, wb_ref, wc_ref, are_ref, aim_ref, s0_ref, d_ref, g_ref, bu_scr, h_scr, st_scr, *, tb):
    tblk = pl.program_id(1)

    @pl.when(tblk == 0)
    def _():
        st_scr[...] = s0_ref[0]

    u = u_ref[...]
    bu_scr[...] = _dot(u.astype(BF16), wb_ref[0])
    st = st_scr[...]
    hr, hi = _s5_scan_steps(bu_scr, h_scr, are_ref[0], aim_ref[0], st[:, :SCAN_COLS], st[:, SCAN_COLS:], tb)
    st_scr[:, :SCAN_COLS] = hr
    st_scr[:, SCAN_COLS:] = hi
    y = _dot(h_scr[...].astype(BF16), wc_ref[0]) + d_ref[...] * u
    g_ref[...] = _gelu(y).astype(g_ref.dtype)


def s5_mixer(xp, g_norm, lam_re, lam_im, log_step, b_re, b_im, c_re, c_im, d_skip, w_glu, tb=128):
    seq = xp.shape[0]
    seg_len = seq // SCAN_SEGS
    tb = min(tb, seg_len)
    rows = tb * SCAN_SEGS
    n_tblk = seg_len // tb
    u = rmsnorm(xp, g_norm, F32)
    disc = _s5_discretise(lam_re, lam_im, log_step, b_re, b_im, seg_len)
    w_b, w_c, a_re, a_im, an_re, an_im = _s5_layout(disc[0], disc[1], disc[2], disc[3], c_re, c_im, disc[4], disc[5])

    u_spec = pl.BlockSpec((rows, SCAN_CH), lambda s, t: (t, s))
    wb_spec = pl.BlockSpec((1, SCAN_CH, 2 * SCAN_COLS), lambda s, t: (s, 0, 0))
    a_spec = pl.BlockSpec((1, SCAN_SEGS, SCAN_COLS), lambda s, t: (s, 0, 0))
    st_spec = pl.BlockSpec((1, SCAN_SEGS, 2 * SCAN_COLS), lambda s, t: (s, 0, 0))

    h_end = pl.pallas_call(
        functools.partial(_s5_state_kernel, tb=tb),
        grid=(N_SUPER, n_tblk),
        in_specs=[u_spec, wb_spec, a_spec, a_spec],
        out_specs=st_spec,
        out_shape=jax.ShapeDtypeStruct((N_SUPER, SCAN_SEGS, 2 * SCAN_COLS), F32),
        scratch_shapes=[pltpu.VMEM((rows, 2 * SCAN_COLS), F32), pltpu.VMEM((SCAN_SEGS, 2 * SCAN_COLS), F32)],
        compiler_params=_params("parallel", "arbitrary"),
        name="s5_state",
    )(u, w_b, a_re, a_im)

    s0 = pl.pallas_call(
        _s5_carry_kernel,
        out_shape=jax.ShapeDtypeStruct((SCAN_SEGS, N_SUPER, 2 * SCAN_COLS), F32),
        name="s5_carry",
    )(h_end.transpose(1, 0, 2), an_re, an_im)
    s0 = s0.transpose(1, 0, 2)

    act = pl.pallas_call(
        functools.partial(_s5_output_kernel, tb=tb),
        grid=(N_SUPER, n_tblk),
        in_specs=[u_spec, wb_spec,
                  pl.BlockSpec((1, 2 * SCAN_COLS, SCAN_CH), lambda s, t: (s, 0, 0)),
                  a_spec, a_spec, st_spec,
                  pl.BlockSpec((1, SCAN_CH), lambda s, t: (0, s))],
        out_specs=pl.BlockSpec((rows, SCAN_CH), lambda s, t: (t, s)),
        out_shape=jax.ShapeDtypeStruct((seq, D_MODEL), BF16),
        scratch_shapes=[pltpu.VMEM((rows, 2 * SCAN_COLS), F32), pltpu.VMEM((rows, 2 * SCAN_COLS), F32),
                        pltpu.VMEM((SCAN_SEGS, 2 * SCAN_COLS), F32)],
        compiler_params=_params("parallel", "arbitrary"),
        name="s5_output",
    )(u, w_b, w_c, a_re, a_im, s0, d_skip.reshape(1, D_MODEL).astype(F32))

    return matmul(act, w_glu, (0, D_MODEL), D_MODEL, _ep_glu_residual, F32, res=xp, name="s5_glu")


def _to_segment_rows(x):
    seq, d = x.shape
    return x.reshape(SCAN_SEGS, seq // SCAN_SEGS, d).transpose(1, 0, 2).reshape(seq, d)


def _from_segment_rows(xp):
    seq, d = xp.shape
    return xp.reshape(seq // SCAN_SEGS, SCAN_SEGS, d).transpose(1, 0, 2).reshape(seq, d)


def _compress_kernel(x_ref, pe_ref, w1_ref, w2_ref, o_ref):
    x = x_ref[0]
    n_half = x.shape[0]
    half = (CMP_BLOCK // 2) * HEAD_DIM
    pe = pe_ref[0]
    xa = (x + pe[0:1, :]).astype(BF16)
    xb = (x + pe[1:2, :]).astype(BF16)
    top = _dot(xa, w1_ref[0, :half, :])
    bot = _dot(xb, w1_ref[0, half:, :])
    bot_next = pltpu.roll(bot, n_half - 1, 0)
    hid = _gelu(top + bot_next)
    out = _dot(hid.astype(BF16), w2_ref[0])
    row = lax.broadcasted_iota(jnp.int32, out.shape, 0)
    o_ref[0] = jnp.where(row < n_half - 1, out, 0.0).astype(o_ref.dtype)


def compress(kv_heads, pe, w1, w2):
    n_hd, n_half, width = kv_heads.shape
    return pl.pallas_call(
        _compress_kernel,
        grid=(n_hd,),
        in_specs=[pl.BlockSpec((1, n_half, width), lambda i: (i, 0, 0)),
                  pl.BlockSpec((1, 2, width), lambda i: (i // N_KV_HEADS, 0, 0)),
                  pl.BlockSpec((1, 2 * width, CMP_HIDDEN), lambda i: (i // N_KV_HEADS, 0, 0)),
                  pl.BlockSpec((1, CMP_HIDDEN, HEAD_DIM), lambda i: (i // N_KV_HEADS, 0, 0))],
        out_specs=pl.BlockSpec((1, n_half, HEAD_DIM), lambda i: (i, 0, 0)),
        out_shape=jax.ShapeDtypeStruct((n_hd, n_half, HEAD_DIM), BF16),
        compiler_params=_params("parallel"),
        name="nsa_compress",
    )(kv_heads, pe, w1, w2)


def _split3_dot(p, w):
    p1 = p.astype(BF16)
    r1 = p - p1.astype(F32)
    p2 = r1.astype(BF16)
    p3 = (r1 - p2.astype(F32)).astype(BF16)
    return _dot(p1, w) + _dot(p2, w) + _dot(p3, w)


def _attn_kernel(q_ref, gate_ref, kc_ref, vc_ref, ks_ref, vs_ref, kw_ref, vw_ref, ov_ref, e_ref, o_ref,
                 qa_scr, acc_scr, m_scr, *, n_sb, n_bias, tiles_per_bias):
    qb = pl.program_id(1)
    q_blk = q_ref[...]
    qr = jnp.concatenate([q_blk[:, n * HEAD_DIM:(n + 1) * HEAD_DIM] for n in range(HEADS_PER_KV)], axis=0)
    t_rel = lax.broadcasted_iota(jnp.int32, (Q_ROWS, 1), 0) & (Q_TILE - 1)
    t_row = qb * Q_TILE + t_rel
    ones_v = jnp.ones((SEL_TILE, HEAD_DIM), BF16)

    n_cmp = kc_ref.shape[1]
    s = _dot_nt(qr, kc_ref[0])
    c_end = lax.broadcasted_iota(jnp.int32, (1, n_cmp), 1) * CMP_STRIDE + (CMP_BLOCK - 1)
    s = jnp.where(c_end <= t_row, s, NEG_INF)
    e = jnp.exp(s - jnp.max(s, axis=1, keepdims=True))
    den = jnp.sum(e, axis=1, keepdims=True)
    has_key = (t_row >= CMP_BLOCK - 1).astype(F32)
    p_c = e * (has_key / den)
    o_c = _dot(p_c.astype(BF16), vc_ref[0])

    p_sum = p_c[0:Q_TILE]
    for n in range(1, HEADS_PER_KV):
        p_sum = p_sum + p_c[n * Q_TILE:(n + 1) * Q_TILE]
    imp = _split3_dot(p_sum, ov_ref[...])
    t_q = qb * Q_TILE + lax.broadcasted_iota(jnp.int32, (Q_TILE, 1), 0)
    blk = t_q >> int(math.log2(SEL_BLOCK))
    j_i = lax.broadcasted_iota(jnp.int32, (1, n_sb), 1)
    j_f = j_i.astype(F32)
    forced = jnp.where(j_i == 0, 1.0, jnp.where(j_i == blk, 1.0, jnp.where(j_i == blk - 1, 1.0, 0.0)))
    score = jnp.where(forced > 0.0, FORCE_SCORE, jnp.where(j_i <= blk, imp, -1.0))
    work = score
    picked = jnp.zeros_like(score)
    for _ in range(min(N_SEL, n_sb)):
        mx = jnp.max(work, axis=1, keepdims=True)
        first = jnp.min(jnp.where(work == mx, j_f, float(n_sb)), axis=1, keepdims=True)
        hit = j_f == first
        picked = jnp.where(hit, 1.0, picked)
        work = jnp.where(hit, -3e38, work)
    before_diag = j_i < 2 * qb
    bias = jnp.where(picked > 0.0, jnp.where(score >= 0.0, jnp.where(before_diag, 0.0, NEG_INF), NEG_INF), NEG_INF)
    bias = bias.astype(BF16)
    for a in range(n_bias):
        qa_scr[a, :, :HEAD_DIM] = qr
        lo = a * BIAS_BLOCKS
        width = min(BIAS_BLOCKS, n_sb - lo)
        slab = bias[:, lo:lo + width]
        if width < BIAS_BLOCKS:
            slab = jnp.concatenate([slab, jnp.full((Q_TILE, BIAS_BLOCKS - width), NEG_INF, BF16)], axis=1)
        for n in range(HEADS_PER_KV):
            qa_scr[a, n * Q_TILE:(n + 1) * Q_TILE, HEAD_DIM:] = slab

    k0 = pl.multiple_of(qb * Q_TILE, Q_TILE)
    s = _dot_nt(qr, ks_ref[pl.ds(k0, Q_TILE), :])
    k_rel = lax.broadcasted_iota(jnp.int32, (1, Q_TILE), 1)
    s = jnp.where(k_rel <= t_rel, s, NEG_INF)
    m0 = jnp.max(s, axis=1, keepdims=True)
    p = jnp.exp(s - m0).astype(BF16)
    acc_scr[...] = _dot(p, jnp.concatenate([vs_ref[pl.ds(k0, Q_TILE), :], ones_v[:Q_TILE]], axis=1))
    m_scr[...] = m0

    def sel_step(i, carry):
        r0 = pl.multiple_of(i * SEL_TILE, SEL_TILE)
        a = i // tiles_per_bias
        e0 = pl.multiple_of((i - a * tiles_per_bias) * SEL_TILE, SEL_TILE)
        k_aug = jnp.concatenate([ks_ref[pl.ds(r0, SEL_TILE), :], e_ref[pl.ds(e0, SEL_TILE), :]], axis=1)
        st = _dot_nt(qa_scr[a], k_aug)
        m_old = m_scr[...]
        m_new = jnp.maximum(m_old, jnp.max(st, axis=1, keepdims=True))
        pt = jnp.exp(st - m_new).astype(BF16)
        v_aug = jnp.concatenate([vs_ref[pl.ds(r0, SEL_TILE), :], ones_v], axis=1)
        acc_scr[...] = jnp.exp(m_old - m_new) * acc_scr[...] + _dot(pt, v_aug)
        m_scr[...] = m_new
        return carry

    n_tiles = (qb * Q_TILE + SEL_TILE - 1) // SEL_TILE
    lax.fori_loop(0, n_tiles, sel_step, 0)
    acc = acc_scr[...]
    o_s = acc[:, :HEAD_DIM] / acc[:, HEAD_DIM:HEAD_DIM + 1]

    band = WINDOW + Q_TILE
    b0 = pl.multiple_of(jnp.maximum(qb * Q_TILE - WINDOW, 0), Q_TILE)
    s = _dot_nt(qr, kw_ref[pl.ds(b0, band), :])
    k_pos = b0 + lax.broadcasted_iota(jnp.int32, (1, band), 1)
    s = jnp.where(k_pos <= t_row, jnp.where(k_pos > t_row - WINDOW, s, NEG_INF), NEG_INF)
    p = jnp.exp(s - jnp.max(s, axis=1, keepdims=True))
    o_w = _dot(p.astype(BF16), vw_ref[pl.ds(b0, band), :]) / jnp.sum(p, axis=1, keepdims=True)

    gates = gate_ref[0]
    for n in range(HEADS_PER_KV):
        rows = slice(n * Q_TILE, (n + 1) * Q_TILE)
        o_n = (gates[0, :, n:n + 1] * o_c[rows] + gates[1, :, n:n + 1] * o_s[rows]
               + gates[2, :, n:n + 1] * o_w[rows])
        o_ref[:, n * HEAD_DIM:(n + 1) * HEAD_DIM] = o_n.astype(o_ref.dtype)


def nsa_attention(q, gates, kv_cmp, kv, overlap, onehot):
    seq = q.shape[0]
    n_cmp = kv_cmp.shape[1]
    n_sb = seq // SEL_BLOCK
    n_bias = pl.cdiv(n_sb, BIAS_BLOCKS)
    e_rows = onehot.shape[0]
    resident = pl.Buffered(1)
    kv_spec = lambda col: pl.BlockSpec((seq, HEAD_DIM), functools.partial(lambda g, i, c: (0, c + g), c=col),
                                       pipeline_mode=resident)
    return pl.pallas_call(
        functools.partial(_attn_kernel, n_sb=n_sb, n_bias=n_bias, tiles_per_bias=e_rows // SEL_TILE),
        grid=(N_KV_HEADS, seq // Q_TILE),
        in_specs=[pl.BlockSpec((Q_TILE, HEADS_PER_KV * HEAD_DIM), lambda g, i: (i, g)),
                  pl.BlockSpec((1, N_BRANCH, Q_TILE, HEADS_PER_KV), lambda g, i: (g, 0, i, 0)),
                  pl.BlockSpec((1, n_cmp, HEAD_DIM), lambda g, i: (g, 0, 0)),
                  pl.BlockSpec((1, n_cmp, HEAD_DIM), lambda g, i: (g + N_KV_HEADS, 0, 0)),
                  kv_spec(0), kv_spec(N_KV_HEADS), kv_spec(2 * N_KV_HEADS), kv_spec(3 * N_KV_HEADS),
                  pl.BlockSpec((n_cmp, n_sb), lambda g, i: (0, 0)),
                  pl.BlockSpec((e_rows, BIAS_BLOCKS), lambda g, i: (0, 0))],
        out_specs=pl.BlockSpec((Q_TILE, HEADS_PER_KV * HEAD_DIM), lambda g, i: (i, g)),
        out_shape=jax.ShapeDtypeStruct((seq, Q_WIDTH), BF16),
        scratch_shapes=[pltpu.VMEM((n_bias, Q_ROWS, 2 * HEAD_DIM), BF16),
                        pltpu.VMEM((Q_ROWS, 2 * HEAD_DIM), F32),
                        pltpu.VMEM((Q_ROWS, 1), F32)],
        compiler_params=_params("parallel", "arbitrary"),
        name="nsa_attention",
    )(q, gates, kv_cmp, kv_cmp, kv, kv, kv, kv, overlap, onehot)


def nsa_mixer(x, g_norm, w_in, w_o, pe, w1, w2):
    seq = x.shape[0]
    h = rmsnorm(x, g_norm, BF16)
    q = matmul(h, w_in[:, :Q_WIDTH], (0,), Q_WIDTH, _ep_scale_q, BF16, name="nsa_q")
    c0 = Q_WIDTH
    kv_c = matmul(h, w_in[:, c0:c0 + 2 * KV_WIDTH], (0,), 2 * KV_WIDTH, _ep_identity, F32, name="nsa_kv_cmp")
    c0 += 2 * KV_WIDTH
    kv = matmul(h, w_in[:, c0:c0 + 4 * KV_WIDTH], (0,), 4 * KV_WIDTH, _ep_identity, BF16, name="nsa_kv")
    c0 += 4 * KV_WIDTH
    n_gate = N_BRANCH * N_HEADS
    w_gate = jnp.pad(w_in[:, c0:c0 + n_gate], ((0, 0), (0, V7X_LANES - n_gate)))
    gates = matmul(h, w_gate, (0,), V7X_LANES, _ep_sigmoid, F32, name="nsa_gates")
    gates = gates[:, :n_gate].reshape(seq, N_KV_HEADS, HEADS_PER_KV, N_BRANCH).transpose(1, 3, 0, 2)

    n_half = seq // CMP_STRIDE
    kv_heads = kv_c.reshape(seq, 2 * N_KV_HEADS, HEAD_DIM).transpose(1, 0, 2)
    kv_heads = kv_heads.reshape(2 * N_KV_HEADS, n_half, CMP_STRIDE * HEAD_DIM)
    kv_cmp = compress(kv_heads, pe, w1, w2)

    n_sb = seq // SEL_BLOCK
    c_start = jnp.arange(n_half) * CMP_STRIDE
    s_start = jnp.arange(n_sb) * SEL_BLOCK
    overlap = ((c_start[:, None] < s_start[None, :] + SEL_BLOCK)
               & (c_start[:, None] + CMP_BLOCK > s_start[None, :])
               & (jnp.arange(n_half)[:, None] < n_half - 1)).astype(BF16)
    e_rows = min(seq, BIAS_KEYS)
    onehot = ((jnp.arange(e_rows)[:, None] // SEL_BLOCK) == jnp.arange(BIAS_BLOCKS)[None, :]).astype(BF16)

    o = nsa_attention(q, gates, kv_cmp, kv, overlap, onehot)
    return matmul(o, w_o, (0,), D_MODEL, _ep_residual, F32, res=x, name="nsa_out")


def kernel(x, norm_g, final_norm_g, ffn_w_in, ffn_w_out, s5_lambda_re, s5_lambda_im, s5_log_step, s5_b_re, s5_b_im, s5_c_re, s5_c_im, s5_d, s5_w_glu, nsa_w_in, nsa_w_o, nsa_pe_k, nsa_w1_k, nsa_w2_k, nsa_pe_v, nsa_w1_v, nsa_w2_v):
    bsz, seq, d = x.shape
    assert bsz == 1 and d == D_MODEL and seq % (SCAN_SEGS * 128) == 0
    xs = x.reshape(seq, d).astype(F32)
    ffn_w_in = ffn_w_in.astype(BF16)
    ffn_w_out = ffn_w_out.astype(BF16)
    s5_w_glu = s5_w_glu.astype(BF16)
    nsa_w_in = nsa_w_in.astype(BF16)
    nsa_w_o = nsa_w_o.astype(BF16)
    half = (CMP_BLOCK // 2) * HEAD_DIM
    pe = jnp.stack([nsa_pe_k, nsa_pe_v], axis=1).astype(F32).reshape(-1, 2, 2, half)
    w1 = jnp.stack([nsa_w1_k, nsa_w1_v], axis=1).astype(BF16)
    w2 = jnp.stack([nsa_w2_k, nsa_w2_v], axis=1).astype(BF16)

    for layer in range(DEPTH):
        j = layer // 2
        if layer % 2 == 0:
            xs = _to_segment_rows(xs)
        xs = ffn(xs, norm_g[layer, 0], ffn_w_in[layer, 0], ffn_w_out[layer, 0])
        if layer % 2 == 0:
            xs = s5_mixer(xs, norm_g[layer, 1], s5_lambda_re[j], s5_lambda_im[j], s5_log_step[j],
                          s5_b_re[j], s5_b_im[j], s5_c_re[j], s5_c_im[j], s5_d[j], s5_w_glu[j])
        else:
            xs = nsa_mixer(xs, norm_g[layer, 1], nsa_w_in[j], nsa_w_o[j], pe[j], w1[j], w2[j])
        xs = ffn(xs, norm_g[layer, 2], ffn_w_in[layer, 1], ffn_w_out[layer, 1])
        if layer % 2 == 0:
            xs = _from_segment_rows(xs)
    return rmsnorm(xs, final_norm_g, x.dtype).reshape(bsz, seq, d)
```

```python
import functools
import math

import jax
import jax.numpy as jnp
from jax import lax
from jax.experimental import pallas as pl
from jax.experimental.pallas import tpu as pltpu

F32 = jnp.float32
BF16 = jnp.bfloat16

D_MODEL = 4096
DEPTH = 4
FFN_DIM = 3 * D_MODEL // 2
NORM_EPS = 1e-6
SSM_GROUP = 16
SSM_GROUPS = D_MODEL // SSM_GROUP
SSM_STATE = 64
HEAD_DIM = 128
N_HEADS = D_MODEL // HEAD_DIM
N_KV_HEADS = 4
HEADS_PER_KV = N_HEADS // N_KV_HEADS
CMP_BLOCK = 32
CMP_STRIDE = 16
CMP_HIDDEN = 256
SEL_BLOCK = 64
N_SEL = 16
WINDOW = 512
N_BRANCH = 3
Q_WIDTH = N_HEADS * HEAD_DIM
KV_WIDTH = N_KV_HEADS * HEAD_DIM
NEG_INF = -1e30
FORCE_SCORE = 1e4

V7X_SUBLANES = 8
V7X_LANES = 128
V7X_VMEM_BYTES = 64 * 1024 * 1024
VMEM_LIMIT = V7X_VMEM_BYTES - 6 * 1024 * 1024

SCAN_SEGS = V7X_SUBLANES
SCAN_SUPER = 8
SCAN_COLS = SCAN_SUPER * SSM_STATE
SCAN_CH = SCAN_SUPER * SSM_GROUP
N_SUPER = SSM_GROUPS // SCAN_SUPER

Q_TILE = 128
Q_ROWS = Q_TILE * HEADS_PER_KV
SEL_TILE = 512
SEL_ROWS = 256
BIAS_BLOCKS = 128
BIAS_KEYS = BIAS_BLOCKS * SEL_BLOCK


def _params(*sem):
    return pltpu.CompilerParams(dimension_semantics=sem, vmem_limit_bytes=VMEM_LIMIT)


def _gelu(x):
    c = math.sqrt(2.0 / math.pi)
    return 0.5 * x * (1.0 + jnp.tanh(c * (x + 0.044715 * (x * x * x))))


def _dot(a, b):
    return jnp.dot(a, b, preferred_element_type=F32)


def _dot_nt(a, b):
    return lax.dot_general(a, b, (((1,), (1,)), ((), ())), preferred_element_type=F32)


def _rmsnorm_kernel(x_ref, g_ref, o_ref):
    x = x_ref[...]
    ms = jnp.mean(x * x, axis=-1, keepdims=True)
    o_ref[...] = ((x * lax.rsqrt(ms + NORM_EPS)) * g_ref[...]).astype(o_ref.dtype)


def rmsnorm(x, g, out_dtype, tm=256):
    m, d = x.shape
    return pl.pallas_call(
        _rmsnorm_kernel,
        grid=(m // tm,),
        in_specs=[pl.BlockSpec((tm, d), lambda i: (i, 0)),
                  pl.BlockSpec((1, d), lambda i: (0, 0))],
        out_specs=pl.BlockSpec((tm, d), lambda i: (i, 0)),
        out_shape=jax.ShapeDtypeStruct((m, d), out_dtype),
        compiler_params=_params("parallel"),
        name="rmsnorm",
    )(x, g.reshape(1, d).astype(F32))


def _mm_kernel(*refs, n_rhs, has_res, epilogue):
    lhs = refs[0][...]
    accs = [_dot(lhs, r[...]) for r in refs[1:1 + n_rhs]]
    res = refs[1 + n_rhs][...] if has_res else None
    o_ref = refs[-1]
    o_ref[...] = epilogue(accs, res).astype(o_ref.dtype)


def matmul(lhs, rhs, rhs_col_offsets, n_out, epilogue, out_dtype, res=None, tm=1024, tn=512, name="matmul"):
    m, k = lhs.shape
    tn = min(tn, n_out)
    tm = min(tm, m)
    n_rhs = len(rhs_col_offsets)
    in_specs = [pl.BlockSpec((tm, k), lambda i, j: (i, 0))]
    for off in rhs_col_offsets:
        in_specs.append(pl.BlockSpec((k, tn), functools.partial(lambda i, j, ob: (0, j + ob), ob=off // tn)))
    args = [lhs] + [rhs] * n_rhs
    if res is not None:
        in_specs.append(pl.BlockSpec((tm, tn), lambda i, j: (i, j)))
        args.append(res)
    return pl.pallas_call(
        functools.partial(_mm_kernel, n_rhs=n_rhs, has_res=res is not None, epilogue=epilogue),
        grid=(m // tm, n_out // tn),
        in_specs=in_specs,
        out_specs=pl.BlockSpec((tm, tn), lambda i, j: (i, j)),
        out_shape=jax.ShapeDtypeStruct((m, n_out), out_dtype),
        compiler_params=_params("parallel", "arbitrary"),
        name=name,
    )(*args)


def _ep_swiglu(accs, res):
    a, b = accs
    return (a * jax.nn.sigmoid(a)) * b


def _ep_half_residual(accs, res):
    return res + 0.5 * accs[0]


def _ep_residual(accs, res):
    return res + accs[0]


def _ep_glu_residual(accs, res):
    val, gate = accs
    return res + val * jax.nn.sigmoid(gate)


def _ep_scale_q(accs, res):
    return accs[0] * (HEAD_DIM ** -0.5 * math.log2(math.e))


def _ep_identity(accs, res):
    return accs[0]


def _ep_sigmoid(accs, res):
    return jax.nn.sigmoid(accs[0])


def ffn(x, g, w_in, w_out):
    h = rmsnorm(x, g, BF16)
    act = matmul(h, w_in, (0, FFN_DIM), FFN_DIM, _ep_swiglu, BF16, name="ffn_in")
    return matmul(act, w_out, (0,), D_MODEL, _ep_half_residual, F32, res=x, name="ffn_out")


def _s5_discretise(lam_re, lam_im, log_step, b_re, b_im, seg_len):
    dt = jnp.exp(log_step.astype(F32))[:, None]
    lr, li = lam_re.astype(F32), lam_im.astype(F32)
    mag = jnp.exp(lr * dt)
    ang = li * dt
    ab_re, ab_im = mag * jnp.cos(ang), mag * jnp.sin(ang)
    den = lr * lr + li * li
    nr, ni = ab_re - 1.0, ab_im
    coef_re = (nr * lr + ni * li) / den
    coef_im = (ni * lr - nr * li) / den
    br, bi = b_re.astype(F32), b_im.astype(F32)
    bb_re = coef_re[..., None] * br - coef_im[..., None] * bi
    bb_im = coef_re[..., None] * bi + coef_im[..., None] * br
    mag_n = jnp.exp(seg_len * (lr * dt))
    ang_n = seg_len * ang
    return ab_re, ab_im, bb_re, bb_im, mag_n * jnp.cos(ang_n), mag_n * jnp.sin(ang_n)


def _s5_layout(ab_re, ab_im, bb_re, bb_im, c_re, c_im, an_re, an_im):
    eye = jnp.eye(SCAN_SUPER, dtype=F32)

    def in_proj(bb):
        w = bb.reshape(N_SUPER, SCAN_SUPER, SSM_STATE, SSM_GROUP).transpose(0, 1, 3, 2)
        w = jnp.einsum('sghp,gk->sghkp', w, eye)
        return w.reshape(N_SUPER, SCAN_CH, SCAN_COLS)

    def out_proj(c):
        w = c.reshape(N_SUPER, SCAN_SUPER, SSM_GROUP, SSM_STATE).transpose(0, 1, 3, 2)
        w = jnp.einsum('sgph,gk->sgpkh', w, eye)
        return w.reshape(N_SUPER, SCAN_COLS, SCAN_CH)

    w_b = jnp.concatenate([in_proj(bb_re), in_proj(bb_im)], axis=2).astype(BF16)
    w_c = jnp.concatenate([out_proj(c_re.astype(F32)), -out_proj(c_im.astype(F32))], axis=1).astype(BF16)

    def lanes(a):
        a = a.reshape(N_SUPER, 1, SCAN_COLS)
        return jnp.broadcast_to(a, (N_SUPER, SCAN_SEGS, SCAN_COLS))

    return w_b, w_c, lanes(ab_re), lanes(ab_im), an_re.reshape(N_SUPER, SCAN_COLS), an_im.reshape(N_SUPER, SCAN_COLS)


def _s5_scan_steps(bu_scr, h_scr, a_re, a_im, h_re, h_im, tb):
    def step(t, carry):
        hr, hi = carry
        r0 = pl.multiple_of(t * SCAN_SEGS, SCAN_SEGS)
        bu = bu_scr[pl.ds(r0, SCAN_SEGS), :]
        nr = (a_re * hr - a_im * hi) + bu[:, :SCAN_COLS]
        ni = (a_re * hi + a_im * hr) + bu[:, SCAN_COLS:]
        if h_scr is not None:
            h_scr[pl.ds(r0, SCAN_SEGS), :SCAN_COLS] = nr
            h_scr[pl.ds(r0, SCAN_SEGS), SCAN_COLS:] = ni
        return nr, ni

    return lax.fori_loop(0, tb, step, (h_re, h_im), unroll=8)


def _s5_state_kernel(u_ref, wb_ref, are_ref, aim_ref, hend_ref, bu_scr, st_scr, *, tb):
    tblk = pl.program_id(1)

    @pl.when(tblk == 0)
    def _():
        st_scr[...] = jnp.zeros_like(st_scr)

    bu_scr[...] = _dot(u_ref[...].astype(BF16), wb_ref[0])
    st = st_scr[...]
    hr, hi = _s5_scan_steps(bu_scr, None, are_ref[0], aim_ref[0], st[:, :SCAN_COLS], st[:, SCAN_COLS:], tb)
    st_scr[:, :SCAN_COLS] = hr
    st_scr[:, SCAN_COLS:] = hi

    @pl.when(tblk == pl.num_programs(1) - 1)
    def _():
        hend_ref[0] = st_scr[...]


def _s5_carry_kernel(hend_ref, an_re_ref, an_im_ref, s_ref):
    ar, ai = an_re_ref[...], an_im_ref[...]
    sr = jnp.zeros_like(ar)
    si = jnp.zeros_like(ar)
    for c in range(SCAN_SEGS):
        s_ref[c, :, :SCAN_COLS] = sr
        s_ref[c, :, SCAN_COLS:] = si
        hr = hend_ref[c, :, :SCAN_COLS]
        hi = hend_ref[c, :, SCAN_COLS:]
        sr, si = (ar * sr - ai * si) + hr, (ar * si + ai * sr) + hi


def _s5_output_kernel(u_ref, wb_ref, wc_ref, are_ref, aim_ref, s0_ref, d_ref, g_ref, bu_scr, h_scr, st_scr, *, tb):
    tblk = pl.program_id(1)

    @pl.when(tblk == 0)
    def _():
        st_scr[...] = s0_ref[0]

    u = u_ref[...]
    bu_scr[...] = _dot(u.astype(BF16), wb_ref[0])
    st = st_scr[...]
    hr, hi = _s5_scan_steps(bu_scr, h_scr, are_ref[0], aim_ref[0], st[:, :SCAN_COLS], st[:, SCAN_COLS:], tb)
    st_scr[:, :SCAN_COLS] = hr
    st_scr[:, SCAN_COLS:] = hi
    y = _dot(h_scr[...].astype(BF16), wc_ref[0]) + d_ref[...] * u
    g_ref[...] = _gelu(y).astype(g_ref.dtype)


def s5_mixer(xp, g_norm, lam_re, lam_im, log_step, b_re, b_im, c_re, c_im, d_skip, w_glu, tb=128):
    seq = xp.shape[0]
    seg_len = seq // SCAN_SEGS
    tb = min(tb, seg_len)
    rows = tb * SCAN_SEGS
    n_tblk = seg_len // tb
    u = rmsnorm(xp, g_norm, F32)
    disc = _s5_discretise(lam_re, lam_im, log_step, b_re, b_im, seg_len)
    w_b, w_c, a_re, a_im, an_re, an_im = _s5_layout(disc[0], disc[1], disc[2], disc[3], c_re, c_im, disc[4], disc[5])

    u_spec = pl.BlockSpec((rows, SCAN_CH), lambda s, t: (t, s))
    wb_spec = pl.BlockSpec((1, SCAN_CH, 2 * SCAN_COLS), lambda s, t: (s, 0, 0))
    a_spec = pl.BlockSpec((1, SCAN_SEGS, SCAN_COLS), lambda s, t: (s, 0, 0))
    st_spec = pl.BlockSpec((1, SCAN_SEGS, 2 * SCAN_COLS), lambda s, t: (s, 0, 0))

    h_end = pl.pallas_call(
        functools.partial(_s5_state_kernel, tb=tb),
        grid=(N_SUPER, n_tblk),
        in_specs=[u_spec, wb_spec, a_spec, a_spec],
        out_specs=st_spec,
        out_shape=jax.ShapeDtypeStruct((N_SUPER, SCAN_SEGS, 2 * SCAN_COLS), F32),
        scratch_shapes=[pltpu.VMEM((rows, 2 * SCAN_COLS), F32), pltpu.VMEM((SCAN_SEGS, 2 * SCAN_COLS), F32)],
        compiler_params=_params("parallel", "arbitrary"),
        name="s5_state",
    )(u, w_b, a_re, a_im)

    s0 = pl.pallas_call(
        _s5_carry_kernel,
        out_shape=jax.ShapeDtypeStruct((SCAN_SEGS, N_SUPER, 2 * SCAN_COLS), F32),
        name="s5_carry",
    )(h_end.transpose(1, 0, 2), an_re, an_im)
    s0 = s0.transpose(1, 0, 2)

    act = pl.pallas_call(
        functools.partial(_s5_output_kernel, tb=tb),
        grid=(N_SUPER, n_tblk),
        in_specs=[u_spec, wb_spec,
                  pl.BlockSpec((1, 2 * SCAN_COLS, SCAN_CH), lambda s, t: (s, 0, 0)),
                  a_spec, a_spec, st_spec,
                  pl.BlockSpec((1, SCAN_CH), lambda s, t: (0, s))],
        out_specs=pl.BlockSpec((rows, SCAN_CH), lambda s, t: (t, s)),
        out_shape=jax.ShapeDtypeStruct((seq, D_MODEL), BF16),
        scratch_shapes=[pltpu.VMEM((rows, 2 * SCAN_COLS), F32), pltpu.VMEM((rows, 2 * SCAN_COLS), F32),
                        pltpu.VMEM((SCAN_SEGS, 2 * SCAN_COLS), F32)],
        compiler_params=_params("parallel", "arbitrary"),
        name="s5_output",
    )(u, w_b, w_c, a_re, a_im, s0, d_skip.reshape(1, D_MODEL).astype(F32))

    return matmul(act, w_glu, (0, D_MODEL), D_MODEL, _ep_glu_residual, F32, res=xp, name="s5_glu")


def _to_segment_rows(x):
    seq, d = x.shape
    return x.reshape(SCAN_SEGS, seq // SCAN_SEGS, d).transpose(1, 0, 2).reshape(seq, d)


def _from_segment_rows(xp):
    seq, d = xp.shape
    return xp.reshape(seq // SCAN_SEGS, SCAN_SEGS, d).transpose(1, 0, 2).reshape(seq, d)


def _compress_kernel(x_ref, pe_ref, w1_ref, w2_ref, o_ref):
    x = x_ref[0]
    n_half = x.shape[0]
    half = (CMP_BLOCK // 2) * HEAD_DIM
    pe = pe_ref[0]
    xa = (x + pe[0:1, :]).astype(BF16)
    xb = (x + pe[1:2, :]).astype(BF16)
    top = _dot(xa, w1_ref[0, :half, :])
    bot = _dot(xb, w1_ref[0, half:, :])
    bot_next = pltpu.roll(bot, n_half - 1, 0)
    hid = _gelu(top + bot_next)
    out = _dot(hid.astype(BF16), w2_ref[0])
    row = lax.broadcasted_iota(jnp.int32, out.shape, 0)
    o_ref[0] = jnp.where(row < n_half - 1, out, 0.0).astype(o_ref.dtype)


def compress(kv_heads, pe, w1, w2):
    n_hd, n_half, width = kv_heads.shape
    return pl.pallas_call(
        _compress_kernel,
        grid=(n_hd,),
        in_specs=[pl.BlockSpec((1, n_half, width), lambda i: (i, 0, 0)),
                  pl.BlockSpec((1, 2, width), lambda i: (i // N_KV_HEADS, 0, 0)),
                  pl.BlockSpec((1, 2 * width, CMP_HIDDEN), lambda i: (i // N_KV_HEADS, 0, 0)),
                  pl.BlockSpec((1, CMP_HIDDEN, HEAD_DIM), lambda i: (i // N_KV_HEADS, 0, 0))],
        out_specs=pl.BlockSpec((1, n_half, HEAD_DIM), lambda i: (i, 0, 0)),
        out_shape=jax.ShapeDtypeStruct((n_hd, n_half, HEAD_DIM), BF16),
        compiler_params=_params("parallel"),
        name="nsa_compress",
    )(kv_heads, pe, w1, w2)


def _split3_dot(p, w):
    p1 = p.astype(BF16)
    r1 = p - p1.astype(F32)
    p2 = r1.astype(BF16)
    p3 = (r1 - p2.astype(F32)).astype(BF16)
    return _dot(p1, w) + _dot(p2, w) + _dot(p3, w)


def _head_rows(q_blk, g):
    base = g * HEADS_PER_KV * HEAD_DIM
    return jnp.concatenate(
        [q_blk[:, base + n * HEAD_DIM:base + (n + 1) * HEAD_DIM] for n in range(HEADS_PER_KV)], axis=0)


def _select_kernel(q_ref, kc_ref, ov_ref, oc_ref, bias_ref, *, n_sb, n_sbp):
    qb = pl.program_id(0)
    q_blk = q_ref[...]
    n_cmp = kc_ref.shape[1]
    t_rel = lax.broadcasted_iota(jnp.int32, (Q_ROWS, 1), 0) & (Q_TILE - 1)
    t_row = qb * Q_TILE + t_rel
    c_end = lax.broadcasted_iota(jnp.int32, (1, n_cmp), 1) * CMP_STRIDE + (CMP_BLOCK - 1)
    visible = c_end <= t_row
    has_key = (t_row >= CMP_BLOCK - 1).astype(F32)
    imps = []
    for g in range(N_KV_HEADS):
        qr = _head_rows(q_blk, g)
        s = jnp.where(visible, _dot_nt(qr, kc_ref[g]), NEG_INF)
        e = jnp.exp2(s - jnp.max(s, axis=1, keepdims=True))
        p_c = e * (has_key / jnp.sum(e, axis=1, keepdims=True))
        o_c = _dot(p_c.astype(BF16), kc_ref[N_KV_HEADS + g])
        for n in range(HEADS_PER_KV):
            col = (g * HEADS_PER_KV + n) * HEAD_DIM
            oc_ref[:, col:col + HEAD_DIM] = o_c[n * Q_TILE:(n + 1) * Q_TILE]
        p_sum = p_c[0:Q_TILE]
        for n in range(1, HEADS_PER_KV):
            p_sum = p_sum + p_c[n * Q_TILE:(n + 1) * Q_TILE]
        imps.append(_split3_dot(p_sum, ov_ref[...]))
    imp = jnp.concatenate(imps, axis=0)

    rows = N_KV_HEADS * Q_TILE
    t_q = qb * Q_TILE + (lax.broadcasted_iota(jnp.int32, (rows, 1), 0) & (Q_TILE - 1))
    blk = t_q >> int(math.log2(SEL_BLOCK))
    j_i = lax.broadcasted_iota(jnp.int32, (1, n_sb), 1)
    forced = jnp.where(j_i == 0, 1.0, jnp.where(j_i == blk, 1.0, jnp.where(j_i == blk - 1, 1.0, 0.0)))
    work = jnp.where(forced > 0.0, -3e38, jnp.where(j_i <= blk, imp, -1.0))
    picked = forced
    for _ in range(min(N_SEL, n_sb) - 3):
        hit = j_i == jnp.argmax(work, axis=1, keepdims=True).astype(jnp.int32)
        picked = jnp.where(hit, 1.0, picked)
        work = jnp.where(hit, -3e38, work)
    bias = jnp.where(picked > 0.0, jnp.where(j_i < 2 * qb, 0.0, NEG_INF), NEG_INF).astype(BF16)
    if n_sbp > n_sb:
        bias = jnp.concatenate([bias, jnp.full((rows, n_sbp - n_sb), NEG_INF, BF16)], axis=1)
    for g in range(N_KV_HEADS):
        bias_ref[:, g * n_sbp:(g + 1) * n_sbp] = bias[g * Q_TILE:(g + 1) * Q_TILE]


def _attend_kernel(q_ref, gate_ref, oc_ref, bias_ref, ks_ref, vs_ref, kw_ref, vw_ref, e_ref, o_ref,
                   qa_scr, acc_scr, m_scr, s_scr, *, n_bias, tiles_per_bias):
    qb = pl.program_id(1)
    qr = _head_rows(q_ref[...], 0)
    t_rel = lax.broadcasted_iota(jnp.int32, (Q_ROWS, 1), 0) & (Q_TILE - 1)
    t_row = qb * Q_TILE + t_rel
    ones_v = jnp.ones((WINDOW + Q_TILE, HEAD_DIM), BF16)

    bias = bias_ref[...]
    for a in range(n_bias):
        qa_scr[a, :, :HEAD_DIM] = qr
        for n in range(HEADS_PER_KV):
            qa_scr[a, n * Q_TILE:(n + 1) * Q_TILE, HEAD_DIM:] = bias[:, a * BIAS_BLOCKS:(a + 1) * BIAS_BLOCKS]

    k0 = pl.multiple_of(qb * Q_TILE, Q_TILE)
    s = _dot_nt(qr, ks_ref[pl.ds(k0, Q_TILE), :])
    k_rel = lax.broadcasted_iota(jnp.int32, (1, Q_TILE), 1)
    s = jnp.where(k_rel <= t_rel, s, NEG_INF)
    m0 = jnp.max(s, axis=1, keepdims=True)
    p = jnp.exp2(s - m0).astype(BF16)
    acc_scr[...] = _dot(p, jnp.concatenate([vs_ref[pl.ds(k0, Q_TILE), :], ones_v[:Q_TILE]], axis=1))
    m_scr[...] = jnp.broadcast_to(m0, (Q_ROWS, V7X_LANES))

    n_tiles = (qb * Q_TILE + SEL_TILE - 1) // SEL_TILE
    chunks = [pl.ds(c * SEL_ROWS, SEL_ROWS) for c in range(Q_ROWS // SEL_ROWS)]

    def keys_of(i):
        r0 = pl.multiple_of(i * SEL_TILE, SEL_TILE)
        a = i // tiles_per_bias
        e0 = pl.multiple_of((i - a * tiles_per_bias) * SEL_TILE, SEL_TILE)
        return a, jnp.concatenate([ks_ref[pl.ds(r0, SEL_TILE), :], e_ref[pl.ds(e0, SEL_TILE), :]], axis=1)

    @pl.when(n_tiles > 0)
    def _():
        a, k_aug = keys_of(0)
        for rows in chunks:
            s_scr[rows, :] = _dot_nt(qa_scr[a, rows, :], k_aug)

    def sel_step(i, carry):
        r0 = pl.multiple_of(i * SEL_TILE, SEL_TILE)
        v_aug = jnp.concatenate([vs_ref[pl.ds(r0, SEL_TILE), :], ones_v[:SEL_TILE]], axis=1)
        a_next, k_next = keys_of(jnp.minimum(i + 1, n_tiles - 1))
        for rows in chunks:
            st = s_scr[rows, :]
            m_old = m_scr[rows, :]
            m_new = jnp.maximum(m_old, jnp.max(st, axis=1, keepdims=True))
            pt = jnp.exp2(st - jnp.concatenate([m_new] * (SEL_TILE // V7X_LANES), axis=1)).astype(BF16)
            alpha = jnp.exp2(m_old - m_new)
            acc_scr[rows, :] = jnp.concatenate([alpha, alpha], axis=1) * acc_scr[rows, :] + _dot(pt, v_aug)
            m_scr[rows, :] = m_new
            s_scr[rows, :] = _dot_nt(qa_scr[a_next, rows, :], k_next)
        return carry

    lax.fori_loop(0, n_tiles, sel_step, 0)
    acc = acc_scr[...]
    o_s = acc[:, :HEAD_DIM] / acc[:, HEAD_DIM:]

    band = WINDOW + Q_TILE
    b0 = pl.multiple_of(jnp.maximum(qb * Q_TILE - WINDOW, 0), Q_TILE)
    s = _dot_nt(qr, kw_ref[pl.ds(b0, band), :])
    k_pos = b0 + lax.broadcasted_iota(jnp.int32, (1, band), 1)
    s = jnp.where(k_pos <= t_row, jnp.where(k_pos > t_row - WINDOW, s, NEG_INF), NEG_INF)
    p = jnp.exp2(s - jnp.max(s, axis=1, keepdims=True)).astype(BF16)
    acc = _dot(p, jnp.concatenate([vw_ref[pl.ds(b0, band), :], ones_v], axis=1))
    o_w = acc[:, :HEAD_DIM] / acc[:, HEAD_DIM:]

    gates = gate_ref[0]
    for n in range(HEADS_PER_KV):
        rows = slice(n * Q_TILE, (n + 1) * Q_TILE)
        cols = slice(n * HEAD_DIM, (n + 1) * HEAD_DIM)
        o_n = (gates[0, :, n:n + 1] * oc_ref[:, cols] + gates[1, :, n:n + 1] * o_s[rows]
               + gates[2, :, n:n + 1] * o_w[rows])
        o_ref[:, cols] = o_n.astype(o_ref.dtype)


def nsa_attention(q, gates, kv_cmp, kv, overlap, onehot):
    seq = q.shape[0]
    n_cmp = kv_cmp.shape[1]
    n_sb = seq // SEL_BLOCK
    n_bias = pl.cdiv(n_sb, BIAS_BLOCKS)
    n_sbp = n_bias * BIAS_BLOCKS
    e_rows = onehot.shape[0]
    group_w = HEADS_PER_KV * HEAD_DIM

    o_cmp, bias = pl.pallas_call(
        functools.partial(_select_kernel, n_sb=n_sb, n_sbp=n_sbp),
        grid=(seq // Q_TILE,),
        in_specs=[pl.BlockSpec((Q_TILE, Q_WIDTH), lambda i: (i, 0)),
                  pl.BlockSpec((2 * N_KV_HEADS, n_cmp, HEAD_DIM), lambda i: (0, 0, 0)),
                  pl.BlockSpec((n_cmp, n_sb), lambda i: (0, 0))],
        out_specs=[pl.BlockSpec((Q_TILE, Q_WIDTH), lambda i: (i, 0)),
                   pl.BlockSpec((Q_TILE, N_KV_HEADS * n_sbp), lambda i: (i, 0))],
        out_shape=[jax.ShapeDtypeStruct((seq, Q_WIDTH), F32),
                   jax.ShapeDtypeStruct((seq, N_KV_HEADS * n_sbp), BF16)],
        compiler_params=_params("parallel"),
        name="nsa_select",
    )(q, kv_cmp, overlap)

    resident = pl.Buffered(1)
    kv_spec = lambda col: pl.BlockSpec((seq, HEAD_DIM), functools.partial(lambda g, i, c: (0, c + g), c=col),
                                       pipeline_mode=resident)
    return pl.pallas_call(
        functools.partial(_attend_kernel, n_bias=n_bias, tiles_per_bias=e_rows // SEL_TILE),
        grid=(N_KV_HEADS, seq // Q_TILE),
        in_specs=[pl.BlockSpec((Q_TILE, group_w), lambda g, i: (i, g)),
                  pl.BlockSpec((1, N_BRANCH, Q_TILE, HEADS_PER_KV), lambda g, i: (g, 0, i, 0)),
                  pl.BlockSpec((Q_TILE, group_w), lambda g, i: (i, g)),
                  pl.BlockSpec((Q_TILE, n_sbp), lambda g, i: (i, g)),
                  kv_spec(0), kv_spec(N_KV_HEADS), kv_spec(2 * N_KV_HEADS), kv_spec(3 * N_KV_HEADS),
                  pl.BlockSpec((e_rows, BIAS_BLOCKS), lambda g, i: (0, 0))],
        out_specs=pl.BlockSpec((Q_TILE, group_w), lambda g, i: (i, g)),
        out_shape=jax.ShapeDtypeStruct((seq, Q_WIDTH), BF16),
        scratch_shapes=[pltpu.VMEM((n_bias, Q_ROWS, 2 * HEAD_DIM), BF16),
                        pltpu.VMEM((Q_ROWS, 2 * HEAD_DIM), F32),
                        pltpu.VMEM((Q_ROWS, V7X_LANES), F32),
                        pltpu.VMEM((Q_ROWS, SEL_TILE), F32)],
        compiler_params=_params("parallel", "arbitrary"),
        name="nsa_attend",
    )(q, gates, o_cmp, bias, kv, kv, kv, kv, onehot)


def nsa_mixer(x, g_norm, w_in, w_o, pe, w1, w2):
    seq = x.shape[0]
    h = rmsnorm(x, g_norm, BF16)
    q = matmul(h, w_in[:, :Q_WIDTH], (0,), Q_WIDTH, _ep_scale_q, BF16, name="nsa_q")
    c0 = Q_WIDTH
    kv_c = matmul(h, w_in[:, c0:c0 + 2 * KV_WIDTH], (0,), 2 * KV_WIDTH, _ep_identity, F32, name="nsa_kv_cmp")
    c0 += 2 * KV_WIDTH
    kv = matmul(h, w_in[:, c0:c0 + 4 * KV_WIDTH], (0,), 4 * KV_WIDTH, _ep_identity, BF16, name="nsa_kv")
    c0 += 4 * KV_WIDTH
    n_gate = N_BRANCH * N_HEADS
    w_gate = jnp.pad(w_in[:, c0:c0 + n_gate], ((0, 0), (0, V7X_LANES - n_gate)))
    gates = matmul(h, w_gate, (0,), V7X_LANES, _ep_sigmoid, F32, name="nsa_gates")
    gates = gates[:, :n_gate].reshape(seq, N_KV_HEADS, HEADS_PER_KV, N_BRANCH).transpose(1, 3, 0, 2)

    n_half = seq // CMP_STRIDE
    kv_heads = kv_c.reshape(seq, 2 * N_KV_HEADS, HEAD_DIM).transpose(1, 0, 2)
    kv_heads = kv_heads.reshape(2 * N_KV_HEADS, n_half, CMP_STRIDE * HEAD_DIM)
    kv_cmp = compress(kv_heads, pe, w1, w2)

    n_sb = seq // SEL_BLOCK
    c_start = jnp.arange(n_half) * CMP_STRIDE
    s_start = jnp.arange(n_sb) * SEL_BLOCK
    overlap = ((c_start[:, None] < s_start[None, :] + SEL_BLOCK)
               & (c_start[:, None] + CMP_BLOCK > s_start[None, :])
               & (jnp.arange(n_half)[:, None] < n_half - 1)).astype(BF16)
    e_rows = min(seq, BIAS_KEYS)
    onehot = ((jnp.arange(e_rows)[:, None] // SEL_BLOCK) == jnp.arange(BIAS_BLOCKS)[None, :]).astype(BF16)

    o = nsa_attention(q, gates, kv_cmp, kv, overlap, onehot)
    return matmul(o, w_o, (0,), D_MODEL, _ep_residual, F32, res=x, name="nsa_out")


def kernel(x, norm_g, final_norm_g, ffn_w_in, ffn_w_out, s5_lambda_re, s5_lambda_im, s5_log_step, s5_b_re, s5_b_im, s5_c_re, s5_c_im, s5_d, s5_w_glu, nsa_w_in, nsa_w_o, nsa_pe_k, nsa_w1_k, nsa_w2_k, nsa_pe_v, nsa_w1_v, nsa_w2_v):
    bsz, seq, d = x.shape
    assert bsz == 1 and d == D_MODEL and seq % (SCAN_SEGS * 128) == 0
    xs = x.reshape(seq, d).astype(F32)
    ffn_w_in = ffn_w_in.astype(BF16)
    ffn_w_out = ffn_w_out.astype(BF16)
    s5_w_glu = s5_w_glu.astype(BF16)
    nsa_w_in = nsa_w_in.astype(BF16)
    nsa_w_o = nsa_w_o.astype(BF16)
    half = (CMP_BLOCK // 2) * HEAD_DIM
    pe = jnp.stack([nsa_pe_k, nsa_pe_v], axis=1).astype(F32).reshape(-1, 2, 2, half)
    w1 = jnp.stack([nsa_w1_k, nsa_w1_v], axis=1).astype(BF16)
    w2 = jnp.stack([nsa_w2_k, nsa_w2_v], axis=1).astype(BF16)

    for layer in range(DEPTH):
        j = layer // 2
        if layer % 2 == 0:
            xs = _to_segment_rows(xs)
        xs = ffn(xs, norm_g[layer, 0], ffn_w_in[layer, 0], ffn_w_out[layer, 0])
        if layer % 2 == 0:
            xs = s5_mixer(xs, norm_g[layer, 1], s5_lambda_re[j], s5_lambda_im[j], s5_log_step[j],
                          s5_b_re[j], s5_b_im[j], s5_c_re[j], s5_c_im[j], s5_d[j], s5_w_glu[j])
        else:
            xs = nsa_mixer(xs, norm_g[layer, 1], nsa_w_in[j], nsa_w_o[j], pe[j], w1[j], w2[j])
        xs = ffn(xs, norm_g[layer, 2], ffn_w_in[layer, 1], ffn_w_out[layer, 1])
        if layer % 2 == 0:
            xs = _from_segment_rows(xs)
    return rmsnorm(xs, final_norm_g, x.dtype).reshape(bsz, seq, d)
```

```python
import functools
import math

import jax
import jax.numpy as jnp
from jax import lax
from jax.experimental import pallas as pl
from jax.experimental.pallas import tpu as pltpu

F32 = jnp.float32
BF16 = jnp.bfloat16

D_MODEL = 4096
DEPTH = 4
FFN_DIM = 3 * D_MODEL // 2
NORM_EPS = 1e-6
SSM_GROUP = 16
SSM_GROUPS = D_MODEL // SSM_GROUP
SSM_STATE = 64
HEAD_DIM = 128
N_HEADS = D_MODEL // HEAD_DIM
N_KV_HEADS = 4
HEADS_PER_KV = N_HEADS // N_KV_HEADS
CMP_BLOCK = 32
CMP_STRIDE = 16
CMP_HIDDEN = 256
SEL_BLOCK = 64
N_SEL = 16
WINDOW = 512
N_BRANCH = 3
Q_WIDTH = N_HEADS * HEAD_DIM
KV_WIDTH = N_KV_HEADS * HEAD_DIM
NEG_INF = -1e30
FORCE_SCORE = 1e4

V7X_SUBLANES = 8
V7X_LANES = 128
V7X_VMEM_BYTES = 64 * 1024 * 1024
VMEM_LIMIT = V7X_VMEM_BYTES - 6 * 1024 * 1024

SCAN_SEGS = V7X_SUBLANES
SCAN_SUPER = 8
SCAN_COLS = SCAN_SUPER * SSM_STATE
SCAN_CH = SCAN_SUPER * SSM_GROUP
N_SUPER = SSM_GROUPS // SCAN_SUPER

Q_TILE = 128
Q_ROWS = Q_TILE * HEADS_PER_KV
SEL_TILE = 1024
CMP_BUCKET = 256
SEL_ROWS = 256
BIAS_BLOCKS = 128
BIAS_KEYS = BIAS_BLOCKS * SEL_BLOCK


def _params(*sem):
    return pltpu.CompilerParams(dimension_semantics=sem, vmem_limit_bytes=VMEM_LIMIT)


def _gelu(x):
    c = math.sqrt(2.0 / math.pi)
    return 0.5 * x * (1.0 + jnp.tanh(c * (x + 0.044715 * (x * x * x))))


def _dot(a, b):
    return jnp.dot(a, b, preferred_element_type=F32)


def _dot_nt(a, b):
    return lax.dot_general(a, b, (((1,), (1,)), ((), ())), preferred_element_type=F32)


def _rmsnorm_kernel(x_ref, g_ref, o_ref):
    x = x_ref[...]
    ms = jnp.mean(x * x, axis=-1, keepdims=True)
    o_ref[...] = ((x * lax.rsqrt(ms + NORM_EPS)) * g_ref[...]).astype(o_ref.dtype)


def rmsnorm(x, g, out_dtype, tm=256):
    m, d = x.shape
    return pl.pallas_call(
        _rmsnorm_kernel,
        grid=(m // tm,),
        in_specs=[pl.BlockSpec((tm, d), lambda i: (i, 0)),
                  pl.BlockSpec((1, d), lambda i: (0, 0))],
        out_specs=pl.BlockSpec((tm, d), lambda i: (i, 0)),
        out_shape=jax.ShapeDtypeStruct((m, d), out_dtype),
        compiler_params=_params("parallel"),
        name="rmsnorm",
    )(x, g.reshape(1, d).astype(F32))


def _mm_kernel(*refs, n_rhs, has_res, epilogue):
    lhs = refs[0][...]
    accs = [_dot(lhs, r[...]) for r in refs[1:1 + n_rhs]]
    res = refs[1 + n_rhs][...] if has_res else None
    o_ref = refs[-1]
    o_ref[...] = epilogue(accs, res).astype(o_ref.dtype)


def matmul(lhs, rhs, rhs_col_offsets, n_out, epilogue, out_dtype, res=None, tm=1024, tn=512, name="matmul"):
    m, k = lhs.shape
    tn = min(tn, n_out)
    tm = min(tm, m)
    n_rhs = len(rhs_col_offsets)
    in_specs = [pl.BlockSpec((tm, k), lambda i, j: (i, 0))]
    for off in rhs_col_offsets:
        in_specs.append(pl.BlockSpec((k, tn), functools.partial(lambda i, j, ob: (0, j + ob), ob=off // tn)))
    args = [lhs] + [rhs] * n_rhs
    if res is not None:
        in_specs.append(pl.BlockSpec((tm, tn), lambda i, j: (i, j)))
        args.append(res)
    return pl.pallas_call(
        functools.partial(_mm_kernel, n_rhs=n_rhs, has_res=res is not None, epilogue=epilogue),
        grid=(m // tm, n_out // tn),
        in_specs=in_specs,
        out_specs=pl.BlockSpec((tm, tn), lambda i, j: (i, j)),
        out_shape=jax.ShapeDtypeStruct((m, n_out), out_dtype),
        compiler_params=_params("parallel", "arbitrary"),
        name=name,
    )(*args)


def _ep_swiglu(accs, res):
    a, b = accs
    return (a * jax.nn.sigmoid(a)) * b


def _ep_half_residual(accs, res):
    return res + 0.5 * accs[0]


def _ep_residual(accs, res):
    return res + accs[0]


def _ep_glu_residual(accs, res):
    val, gate = accs
    return res + val * jax.nn.sigmoid(gate)


def _ep_scale_q(accs, res):
    return accs[0] * (HEAD_DIM ** -0.5 * math.log2(math.e))


def _ep_identity(accs, res):
    return accs[0]


def _ep_sigmoid(accs, res):
    return jax.nn.sigmoid(accs[0])


def ffn(x, g, w_in, w_out):
    h = rmsnorm(x, g, BF16)
    act = matmul(h, w_in, (0, FFN_DIM), FFN_DIM, _ep_swiglu, BF16, name="ffn_in")
    return matmul(act, w_out, (0,), D_MODEL, _ep_half_residual, F32, res=x, name="ffn_out")


def _s5_discretise(lam_re, lam_im, log_step, b_re, b_im, seg_len):
    dt = jnp.exp(log_step.astype(F32))[:, None]
    lr, li = lam_re.astype(F32), lam_im.astype(F32)
    mag = jnp.exp(lr * dt)
    ang = li * dt
    ab_re, ab_im = mag * jnp.cos(ang), mag * jnp.sin(ang)
    den = lr * lr + li * li
    nr, ni = ab_re - 1.0, ab_im
    coef_re = (nr * lr + ni * li) / den
    coef_im = (ni * lr - nr * li) / den
    br, bi = b_re.astype(F32), b_im.astype(F32)
    bb_re = coef_re[..., None] * br - coef_im[..., None] * bi
    bb_im = coef_re[..., None] * bi + coef_im[..., None] * br
    mag_n = jnp.exp(seg_len * (lr * dt))
    ang_n = seg_len * ang
    return ab_re, ab_im, bb_re, bb_im, mag_n * jnp.cos(ang_n), mag_n * jnp.sin(ang_n)


def _s5_layout(ab_re, ab_im, bb_re, bb_im, c_re, c_im, an_re, an_im):
    eye = jnp.eye(SCAN_SUPER, dtype=F32)

    def in_proj(bb):
        w = bb.reshape(N_SUPER, SCAN_SUPER, SSM_STATE, SSM_GROUP).transpose(0, 1, 3, 2)
        w = jnp.einsum('sghp,gk->sghkp', w, eye)
        return w.reshape(N_SUPER, SCAN_CH, SCAN_COLS)

    def out_proj(c):
        w = c.reshape(N_SUPER, SCAN_SUPER, SSM_GROUP, SSM_STATE).transpose(0, 1, 3, 2)
        w = jnp.einsum('sgph,gk->sgpkh', w, eye)
        return w.reshape(N_SUPER, SCAN_COLS, SCAN_CH)

    w_b = jnp.concatenate([in_proj(bb_re), in_proj(bb_im)], axis=2).astype(BF16)
    w_c = jnp.concatenate([out_proj(c_re.astype(F32)), -out_proj(c_im.astype(F32))], axis=1).astype(BF16)

    def lanes(a):
        a = a.reshape(N_SUPER, 1, SCAN_COLS)
        return jnp.broadcast_to(a, (N_SUPER, SCAN_SEGS, SCAN_COLS))

    return w_b, w_c, lanes(ab_re), lanes(ab_im), an_re.reshape(N_SUPER, SCAN_COLS), an_im.reshape(N_SUPER, SCAN_COLS)


def _s5_scan_steps(bu_scr, h_scr, a_re, a_im, h_re, h_im, tb):
    def step(t, carry):
        hr, hi = carry
        r0 = pl.multiple_of(t * SCAN_SEGS, SCAN_SEGS)
        bu = bu_scr[pl.ds(r0, SCAN_SEGS), :]
        nr = (a_re * hr - a_im * hi) + bu[:, :SCAN_COLS]
        ni = (a_re * hi + a_im * hr) + bu[:, SCAN_COLS:]
        if h_scr is not None:
            h_scr[pl.ds(r0, SCAN_SEGS), :SCAN_COLS] = nr
            h_scr[pl.ds(r0, SCAN_SEGS), SCAN_COLS:] = ni
        return nr, ni

    return lax.fori_loop(0, tb, step, (h_re, h_im), unroll=8)


def _s5_state_kernel(u_ref, wb_ref, are_ref, aim_ref, hend_ref, bu_scr, st_scr, *, tb):
    tblk = pl.program_id(1)

    @pl.when(tblk == 0)
    def _():
        st_scr[...] = jnp.zeros_like(st_scr)

    bu_scr[...] = _dot(u_ref[...].astype(BF16), wb_ref[0])
    st = st_scr[...]
    hr, hi = _s5_scan_steps(bu_scr, None, are_ref[0], aim_ref[0], st[:, :SCAN_COLS], st[:, SCAN_COLS:], tb)
    st_scr[:, :SCAN_COLS] = hr
    st_scr[:, SCAN_COLS:] = hi

    @pl.when(tblk == pl.num_programs(1) - 1)
    def _():
        hend_ref[0] = st_scr[...]


def _s5_carry_kernel(hend_ref, an_re_ref, an_im_ref, s_ref):
    ar, ai = an_re_ref[...], an_im_ref[...]
    sr = jnp.zeros_like(ar)
    si = jnp.zeros_like(ar)
    for c in range(SCAN_SEGS):
        s_ref[c, :, :SCAN_COLS] = sr
        s_ref[c, :, SCAN_COLS:] = si
        hr = hend_ref[c, :, :SCAN_COLS]
        hi = hend_ref[c, :, SCAN_COLS:]
        sr, si = (ar * sr - ai * si) + hr, (ar * si + ai * sr) + hi


def _s5_output_kernel(u_ref, wb_ref, wc_ref, are_ref, aim_ref, s0_ref, d_ref, g_ref, bu_scr, h_scr, st_scr, *, tb):
    tblk = pl.program_id(1)

    @pl.when(tblk == 0)
    def _():
        st_scr[...] = s0_ref[0]

    u = u_ref[...]
    bu_scr[...] = _dot(u.astype(BF16), wb_ref[0])
    st = st_scr[...]
    hr, hi = _s5_scan_steps(bu_scr, h_scr, are_ref[0], aim_ref[0], st[:, :SCAN_COLS], st[:, SCAN_COLS:], tb)
    st_scr[:, :SCAN_COLS] = hr
    st_scr[:, SCAN_COLS:] = hi
    y = _dot(h_scr[...].astype(BF16), wc_ref[0]) + d_ref[...] * u
    g_ref[...] = _gelu(y).astype(g_ref.dtype)


def s5_mixer(xp, g_norm, lam_re, lam_im, log_step, b_re, b_im, c_re, c_im, d_skip, w_glu, tb=128):
    seq = xp.shape[0]
    seg_len = seq // SCAN_SEGS
    tb = min(tb, seg_len)
    rows = tb * SCAN_SEGS
    n_tblk = seg_len // tb
    u = rmsnorm(xp, g_norm, F32)
    disc = _s5_discretise(lam_re, lam_im, log_step, b_re, b_im, seg_len)
    w_b, w_c, a_re, a_im, an_re, an_im = _s5_layout(disc[0], disc[1], disc[2], disc[3], c_re, c_im, disc[4], disc[5])

    u_spec = pl.BlockSpec((rows, SCAN_CH), lambda s, t: (t, s))
    wb_spec = pl.BlockSpec((1, SCAN_CH, 2 * SCAN_COLS), lambda s, t: (s, 0, 0))
    a_spec = pl.BlockSpec((1, SCAN_SEGS, SCAN_COLS), lambda s, t: (s, 0, 0))
    st_spec = pl.BlockSpec((1, SCAN_SEGS, 2 * SCAN_COLS), lambda s, t: (s, 0, 0))

    h_end = pl.pallas_call(
        functools.partial(_s5_state_kernel, tb=tb),
        grid=(N_SUPER, n_tblk),
        in_specs=[u_spec, wb_spec, a_spec, a_spec],
        out_specs=st_spec,
        out_shape=jax.ShapeDtypeStruct((N_SUPER, SCAN_SEGS, 2 * SCAN_COLS), F32),
        scratch_shapes=[pltpu.VMEM((rows, 2 * SCAN_COLS), F32), pltpu.VMEM((SCAN_SEGS, 2 * SCAN_COLS), F32)],
        compiler_params=_params("parallel", "arbitrary"),
        name="s5_state",
    )(u, w_b, a_re, a_im)

    s0 = pl.pallas_call(
        _s5_carry_kernel,
        out_shape=jax.ShapeDtypeStruct((SCAN_SEGS, N_SUPER, 2 * SCAN_COLS), F32),
        name="s5_carry",
    )(h_end.transpose(1, 0, 2), an_re, an_im)
    s0 = s0.transpose(1, 0, 2)

    act = pl.pallas_call(
        functools.partial(_s5_output_kernel, tb=tb),
        grid=(N_SUPER, n_tblk),
        in_specs=[u_spec, wb_spec,
                  pl.BlockSpec((1, 2 * SCAN_COLS, SCAN_CH), lambda s, t: (s, 0, 0)),
                  a_spec, a_spec, st_spec,
                  pl.BlockSpec((1, SCAN_CH), lambda s, t: (0, s))],
        out_specs=pl.BlockSpec((rows, SCAN_CH), lambda s, t: (t, s)),
        out_shape=jax.ShapeDtypeStruct((seq, D_MODEL), BF16),
        scratch_shapes=[pltpu.VMEM((rows, 2 * SCAN_COLS), F32), pltpu.VMEM((rows, 2 * SCAN_COLS), F32),
                        pltpu.VMEM((SCAN_SEGS, 2 * SCAN_COLS), F32)],
        compiler_params=_params("parallel", "arbitrary"),
        name="s5_output",
    )(u, w_b, w_c, a_re, a_im, s0, d_skip.reshape(1, D_MODEL).astype(F32))

    return matmul(act, w_glu, (0, D_MODEL), D_MODEL, _ep_glu_residual, F32, res=xp, name="s5_glu")


def _to_segment_rows(x):
    seq, d = x.shape
    return x.reshape(SCAN_SEGS, seq // SCAN_SEGS, d).transpose(1, 0, 2).reshape(seq, d)


def _from_segment_rows(xp):
    seq, d = xp.shape
    return xp.reshape(seq // SCAN_SEGS, SCAN_SEGS, d).transpose(1, 0, 2).reshape(seq, d)


def _compress_kernel(x_ref, pe_ref, w1_ref, w2_ref, o_ref):
    x = x_ref[0]
    n_half = x.shape[0]
    half = (CMP_BLOCK // 2) * HEAD_DIM
    pe = pe_ref[0]
    xa = (x + pe[0:1, :]).astype(BF16)
    xb = (x + pe[1:2, :]).astype(BF16)
    top = _dot(xa, w1_ref[0, :half, :])
    bot = _dot(xb, w1_ref[0, half:, :])
    bot_next = pltpu.roll(bot, n_half - 1, 0)
    hid = _gelu(top + bot_next)
    out = _dot(hid.astype(BF16), w2_ref[0])
    row = lax.broadcasted_iota(jnp.int32, out.shape, 0)
    o_ref[0] = jnp.where(row < n_half - 1, out, 0.0).astype(o_ref.dtype)


def compress(kv_heads, pe, w1, w2):
    n_hd, n_half, width = kv_heads.shape
    return pl.pallas_call(
        _compress_kernel,
        grid=(n_hd,),
        in_specs=[pl.BlockSpec((1, n_half, width), lambda i: (i, 0, 0)),
                  pl.BlockSpec((1, 2, width), lambda i: (i // N_KV_HEADS, 0, 0)),
                  pl.BlockSpec((1, 2 * width, CMP_HIDDEN), lambda i: (i // N_KV_HEADS, 0, 0)),
                  pl.BlockSpec((1, CMP_HIDDEN, HEAD_DIM), lambda i: (i // N_KV_HEADS, 0, 0))],
        out_specs=pl.BlockSpec((1, n_half, HEAD_DIM), lambda i: (i, 0, 0)),
        out_shape=jax.ShapeDtypeStruct((n_hd, n_half, HEAD_DIM), BF16),
        compiler_params=_params("parallel"),
        name="nsa_compress",
    )(kv_heads, pe, w1, w2)


def _split3_dot(p, w):
    p1 = p.astype(BF16)
    r1 = p - p1.astype(F32)
    p2 = r1.astype(BF16)
    p3 = (r1 - p2.astype(F32)).astype(BF16)
    return _dot(p1, w) + _dot(p2, w) + _dot(p3, w)


def _head_rows(q_blk, g):
    base = g * HEADS_PER_KV * HEAD_DIM
    return jnp.concatenate(
        [q_blk[:, base + n * HEAD_DIM:base + (n + 1) * HEAD_DIM] for n in range(HEADS_PER_KV)], axis=0)


def _select_kernel(q_ref, kc_ref, ov_ref, oc_ref, bias_ref, imp_scr, *, n_sb, n_sbp, cmp_bucket):
    qb = pl.program_id(0)
    n_cmp = kc_ref.shape[1]
    t_rel = lax.broadcasted_iota(jnp.int32, (Q_ROWS, 1), 0) & (Q_TILE - 1)
    t_row = qb * Q_TILE + t_rel
    has_key = (t_row >= CMP_BLOCK - 1).astype(F32)

    def compressed_branch(n_cols):
        q_blk = q_ref[...]
        c_end = lax.broadcasted_iota(jnp.int32, (1, n_cols), 1) * CMP_STRIDE + (CMP_BLOCK - 1)
        visible = c_end <= t_row
        for g in range(N_KV_HEADS):
            qr = _head_rows(q_blk, g)
            s = jnp.where(visible, _dot_nt(qr, kc_ref[g, :n_cols, :]), NEG_INF)
            e = jnp.exp2(s - jnp.max(s, axis=1, keepdims=True))
            p_c = e * (has_key / jnp.sum(e, axis=1, keepdims=True))
            o_c = _dot(p_c.astype(BF16), kc_ref[N_KV_HEADS + g, :n_cols, :])
            for n in range(HEADS_PER_KV):
                col = (g * HEADS_PER_KV + n) * HEAD_DIM
                oc_ref[:, col:col + HEAD_DIM] = o_c[n * Q_TILE:(n + 1) * Q_TILE]
            p_sum = p_c[0:Q_TILE]
            for n in range(1, HEADS_PER_KV):
                p_sum = p_sum + p_c[n * Q_TILE:(n + 1) * Q_TILE]
            imp_scr[g * Q_TILE:(g + 1) * Q_TILE, :] = _split3_dot(p_sum, ov_ref[:n_cols, :])

    last_visible = (Q_TILE // CMP_STRIDE) * qb + (Q_TILE - CMP_BLOCK) // CMP_STRIDE
    for k in range(n_cmp // cmp_bucket):
        pl.when(last_visible // cmp_bucket == k)(functools.partial(compressed_branch, (k + 1) * cmp_bucket))
    imp = imp_scr[...]

    rows = N_KV_HEADS * Q_TILE
    t_q = qb * Q_TILE + (lax.broadcasted_iota(jnp.int32, (rows, 1), 0) & (Q_TILE - 1))
    blk = t_q >> int(math.log2(SEL_BLOCK))
    j_i = lax.broadcasted_iota(jnp.int32, (1, n_sb), 1)
    forced = jnp.where(j_i == 0, 1.0, jnp.where(j_i == blk, 1.0, jnp.where(j_i == blk - 1, 1.0, 0.0)))
    work = jnp.where(forced > 0.0, -3e38, jnp.where(j_i <= blk, imp, -1.0))
    picked = forced
    for _ in range(min(N_SEL, n_sb) - 3):
        hit = j_i == jnp.argmax(work, axis=1, keepdims=True).astype(jnp.int32)
        picked = jnp.where(hit, 1.0, picked)
        work = jnp.where(hit, -3e38, work)
    bias = jnp.where(picked > 0.0, jnp.where(j_i < 2 * qb, 0.0, NEG_INF), NEG_INF).astype(BF16)
    if n_sbp > n_sb:
        bias = jnp.concatenate([bias, jnp.full((rows, n_sbp - n_sb), NEG_INF, BF16)], axis=1)
    for g in range(N_KV_HEADS):
        bias_ref[:, g * n_sbp:(g + 1) * n_sbp] = bias[g * Q_TILE:(g + 1) * Q_TILE]


def _attend_kernel(q_ref, gate_ref, oc_ref, bias_ref, ks_ref, vs_ref, kw_ref, vw_ref, e_ref, o_ref,
                   qa_scr, acc_scr, m_scr, s_scr, ow_scr, *, n_bias, tiles_per_bias):
    qb = pl.program_id(1)
    qr = _head_rows(q_ref[...], 0)
    t_rel = lax.broadcasted_iota(jnp.int32, (Q_ROWS, 1), 0) & (Q_TILE - 1)
    t_row = qb * Q_TILE + t_rel
    ones_v = jnp.ones((max(WINDOW + Q_TILE, SEL_TILE), HEAD_DIM), BF16)

    bias = bias_ref[...]
    for a in range(n_bias):
        qa_scr[a, :, :HEAD_DIM] = qr
        for n in range(HEADS_PER_KV):
            qa_scr[a, n * Q_TILE:(n + 1) * Q_TILE, HEAD_DIM:] = bias[:, a * BIAS_BLOCKS:(a + 1) * BIAS_BLOCKS]

    k0 = pl.multiple_of(qb * Q_TILE, Q_TILE)
    s = _dot_nt(qr, ks_ref[pl.ds(k0, Q_TILE), :])
    k_rel = lax.broadcasted_iota(jnp.int32, (1, Q_TILE), 1)
    s = jnp.where(k_rel <= t_rel, s, NEG_INF)
    m0 = jnp.max(s, axis=1, keepdims=True)
    p = jnp.exp2(s - m0).astype(BF16)
    acc_scr[...] = _dot(p, jnp.concatenate([vs_ref[pl.ds(k0, Q_TILE), :], ones_v[:Q_TILE]], axis=1))
    m_scr[...] = jnp.broadcast_to(m0, (Q_ROWS, V7X_LANES))

    band = WINDOW + Q_TILE
    b0 = pl.multiple_of(jnp.maximum(qb * Q_TILE - WINDOW, 0), Q_TILE)
    s = _dot_nt(qr, kw_ref[pl.ds(b0, band), :])
    k_pos = b0 + lax.broadcasted_iota(jnp.int32, (1, band), 1)
    s = jnp.where(k_pos <= t_row, jnp.where(k_pos > t_row - WINDOW, s, NEG_INF), NEG_INF)
    p = jnp.exp2(s - jnp.max(s, axis=1, keepdims=True)).astype(BF16)
    acc_w = _dot(p, jnp.concatenate([vw_ref[pl.ds(b0, band), :], ones_v[:band]], axis=1))
    ow_scr[...] = acc_w[:, :HEAD_DIM] / acc_w[:, HEAD_DIM:]

    n_tiles = (qb * Q_TILE + SEL_TILE - 1) // SEL_TILE
    chunks = [pl.ds(c * SEL_ROWS, SEL_ROWS) for c in range(Q_ROWS // SEL_ROWS)]

    def keys_of(i):
        r0 = pl.multiple_of(i * SEL_TILE, SEL_TILE)
        a = i // tiles_per_bias
        e0 = pl.multiple_of((i - a * tiles_per_bias) * SEL_TILE, SEL_TILE)
        return a, jnp.concatenate([ks_ref[pl.ds(r0, SEL_TILE), :], e_ref[pl.ds(e0, SEL_TILE), :]], axis=1)

    a_first, k_first = keys_of(0)
    for rows in chunks:
        s_scr[rows, :] = _dot_nt(qa_scr[a_first, rows, :], k_first)

    def sel_step(i, has_next):
        r0 = pl.multiple_of(i * SEL_TILE, SEL_TILE)
        v_aug = jnp.concatenate([vs_ref[pl.ds(r0, SEL_TILE), :], ones_v[:SEL_TILE]], axis=1)
        if has_next:
            a_next, k_next = keys_of(i + 1)
        for rows in chunks:
            st = s_scr[rows, :]
            m_old = m_scr[rows, :]
            m_new = jnp.maximum(m_old, jnp.max(st, axis=1, keepdims=True))
            pt = jnp.exp2(st - jnp.concatenate([m_new] * (SEL_TILE // V7X_LANES), axis=1)).astype(BF16)
            alpha = jnp.exp2(m_old - m_new)
            acc_scr[rows, :] = jnp.concatenate([alpha, alpha], axis=1) * acc_scr[rows, :] + _dot(pt, v_aug)
            m_scr[rows, :] = m_new
            if has_next:
                s_scr[rows, :] = _dot_nt(qa_scr[a_next, rows, :], k_next)

    def sel_loop_body(i, carry):
        sel_step(i, True)
        return carry

    lax.fori_loop(0, n_tiles - 1, sel_loop_body, 0)

    @pl.when(n_tiles > 0)
    def _():
        sel_step(n_tiles - 1, False)

    gates = gate_ref[0]
    for n in range(HEADS_PER_KV):
        rows = pl.ds(n * Q_TILE, Q_TILE)
        cols = slice(n * HEAD_DIM, (n + 1) * HEAD_DIM)
        o_s = acc_scr[rows, :HEAD_DIM] / acc_scr[rows, HEAD_DIM:]
        o_n = (gates[0, :, n:n + 1] * oc_ref[:, cols] + gates[1, :, n:n + 1] * o_s
               + gates[2, :, n:n + 1] * ow_scr[rows, :])
        o_ref[:, cols] = o_n.astype(o_ref.dtype)


def nsa_attention(q, gates, kv_cmp, kv, overlap, onehot):
    seq = q.shape[0]
    n_cmp = kv_cmp.shape[1]
    n_sb = seq // SEL_BLOCK
    n_bias = pl.cdiv(n_sb, BIAS_BLOCKS)
    n_sbp = n_bias * BIAS_BLOCKS
    e_rows = onehot.shape[0]
    group_w = HEADS_PER_KV * HEAD_DIM

    o_cmp, bias = pl.pallas_call(
        functools.partial(_select_kernel, n_sb=n_sb, n_sbp=n_sbp, cmp_bucket=min(CMP_BUCKET, n_cmp)),
        grid=(seq // Q_TILE,),
        in_specs=[pl.BlockSpec((Q_TILE, Q_WIDTH), lambda i: (i, 0)),
                  pl.BlockSpec((2 * N_KV_HEADS, n_cmp, HEAD_DIM), lambda i: (0, 0, 0)),
                  pl.BlockSpec((n_cmp, n_sb), lambda i: (0, 0))],
        out_specs=[pl.BlockSpec((Q_TILE, Q_WIDTH), lambda i: (i, 0)),
                   pl.BlockSpec((Q_TILE, N_KV_HEADS * n_sbp), lambda i: (i, 0))],
        out_shape=[jax.ShapeDtypeStruct((seq, Q_WIDTH), F32),
                   jax.ShapeDtypeStruct((seq, N_KV_HEADS * n_sbp), BF16)],
        scratch_shapes=[pltpu.VMEM((N_KV_HEADS * Q_TILE, n_sb), F32)],
        compiler_params=_params("parallel"),
        name="nsa_select",
    )(q, kv_cmp, overlap)

    resident = pl.Buffered(1)
    kv_spec = lambda col: pl.BlockSpec((seq, HEAD_DIM), functools.partial(lambda g, i, c: (0, c + g), c=col),
                                       pipeline_mode=resident)
    return pl.pallas_call(
        functools.partial(_attend_kernel, n_bias=n_bias, tiles_per_bias=e_rows // SEL_TILE),
        grid=(N_KV_HEADS, seq // Q_TILE),
        in_specs=[pl.BlockSpec((Q_TILE, group_w), lambda g, i: (i, g)),
                  pl.BlockSpec((1, N_BRANCH, Q_TILE, HEADS_PER_KV), lambda g, i: (g, 0, i, 0)),
                  pl.BlockSpec((Q_TILE, group_w), lambda g, i: (i, g)),
                  pl.BlockSpec((Q_TILE, n_sbp), lambda g, i: (i, g)),
                  kv_spec(0), kv_spec(N_KV_HEADS), kv_spec(2 * N_KV_HEADS), kv_spec(3 * N_KV_HEADS),
                  pl.BlockSpec((e_rows, BIAS_BLOCKS), lambda g, i: (0, 0))],
        out_specs=pl.BlockSpec((Q_TILE, group_w), lambda g, i: (i, g)),
        out_shape=jax.ShapeDtypeStruct((seq, Q_WIDTH), BF16),
        scratch_shapes=[pltpu.VMEM((n_bias, Q_ROWS, 2 * HEAD_DIM), BF16),
                        pltpu.VMEM((Q_ROWS, 2 * HEAD_DIM), F32),
                        pltpu.VMEM((Q_ROWS, V7X_LANES), F32),
                        pltpu.VMEM((Q_ROWS, SEL_TILE), F32),
                        pltpu.VMEM((Q_ROWS, HEAD_DIM), F32)],
        compiler_params=_params("parallel", "arbitrary"),
        name="nsa_attend",
    )(q, gates, o_cmp, bias, kv, kv, kv, kv, onehot)


def nsa_mixer(x, g_norm, w_in, w_o, pe, w1, w2):
    seq = x.shape[0]
    h = rmsnorm(x, g_norm, BF16)
    q = matmul(h, w_in, (0,), Q_WIDTH, _ep_scale_q, BF16, name="nsa_q")
    c0 = Q_WIDTH
    kv_c = matmul(h, w_in, (c0,), 2 * KV_WIDTH, _ep_identity, F32, name="nsa_kv_cmp")
    c0 += 2 * KV_WIDTH
    kv = matmul(h, w_in, (c0,), 4 * KV_WIDTH, _ep_identity, BF16, name="nsa_kv")
    c0 += 4 * KV_WIDTH
    n_gate = N_BRANCH * N_HEADS
    gates = matmul(h, w_in, (c0,), V7X_LANES, _ep_sigmoid, F32, name="nsa_gates")
    gates = gates[:, :n_gate].reshape(seq, N_KV_HEADS, HEADS_PER_KV, N_BRANCH).transpose(1, 3, 0, 2)

    n_half = seq // CMP_STRIDE
    kv_heads = kv_c.reshape(seq, 2 * N_KV_HEADS, HEAD_DIM).transpose(1, 0, 2)
    kv_heads = kv_heads.reshape(2 * N_KV_HEADS, n_half, CMP_STRIDE * HEAD_DIM)
    kv_cmp = compress(kv_heads, pe, w1, w2)

    n_sb = seq // SEL_BLOCK
    c_start = jnp.arange(n_half) * CMP_STRIDE
    s_start = jnp.arange(n_sb) * SEL_BLOCK
    overlap = ((c_start[:, None] < s_start[None, :] + SEL_BLOCK)
               & (c_start[:, None] + CMP_BLOCK > s_start[None, :])
               & (jnp.arange(n_half)[:, None] < n_half - 1)).astype(BF16)
    e_rows = min(seq, BIAS_KEYS)
    onehot = ((jnp.arange(e_rows)[:, None] // SEL_BLOCK) == jnp.arange(BIAS_BLOCKS)[None, :]).astype(BF16)

    o = nsa_attention(q, gates, kv_cmp, kv, overlap, onehot)
    return matmul(o, w_o, (0,), D_MODEL, _ep_residual, F32, res=x, name="nsa_out")


def kernel(x, norm_g, final_norm_g, ffn_w_in, ffn_w_out, s5_lambda_re, s5_lambda_im, s5_log_step, s5_b_re, s5_b_im, s5_c_re, s5_c_im, s5_d, s5_w_glu, nsa_w_in, nsa_w_o, nsa_pe_k, nsa_w1_k, nsa_w2_k, nsa_pe_v, nsa_w1_v, nsa_w2_v):
    bsz, seq, d = x.shape
    assert bsz == 1 and d == D_MODEL and seq % (SCAN_SEGS * 128) == 0
    xs = x.reshape(seq, d).astype(F32)
    ffn_w_in = ffn_w_in.astype(BF16)
    ffn_w_out = ffn_w_out.astype(BF16)
    s5_w_glu = s5_w_glu.astype(BF16)
    nsa_w_in = nsa_w_in.astype(BF16)
    nsa_w_o = nsa_w_o.astype(BF16)
    half = (CMP_BLOCK // 2) * HEAD_DIM
    pe = jnp.stack([nsa_pe_k, nsa_pe_v], axis=1).astype(F32).reshape(-1, 2, 2, half)
    w1 = jnp.stack([nsa_w1_k, nsa_w1_v], axis=1).astype(BF16)
    w2 = jnp.stack([nsa_w2_k, nsa_w2_v], axis=1).astype(BF16)

    for layer in range(DEPTH):
        j = layer // 2
        if layer % 2 == 0:
            xs = _to_segment_rows(xs)
        xs = ffn(xs, norm_g[layer, 0], ffn_w_in[layer, 0], ffn_w_out[layer, 0])
        if layer % 2 == 0:
            xs = s5_mixer(xs, norm_g[layer, 1], s5_lambda_re[j], s5_lambda_im[j], s5_log_step[j],
                          s5_b_re[j], s5_b_im[j], s5_c_re[j], s5_c_im[j], s5_d[j], s5_w_glu[j])
        else:
            xs = nsa_mixer(xs, norm_g[layer, 1], nsa_w_in[j], nsa_w_o[j], pe[j], w1[j], w2[j])
        xs = ffn(xs, norm_g[layer, 2], ffn_w_in[layer, 1], ffn_w_out[layer, 1])
        if layer % 2 == 0:
            xs = _from_segment_rows(xs)
    return rmsnorm(xs, final_norm_g, x.dtype).reshape(bsz, seq, d)
```

```python
import functools
import math

import jax
import jax.numpy as jnp
from jax import lax
from jax.experimental import pallas as pl
from jax.experimental.pallas import tpu as pltpu

F32 = jnp.float32
BF16 = jnp.bfloat16

D_MODEL = 4096
DEPTH = 4
FFN_DIM = 3 * D_MODEL // 2
NORM_EPS = 1e-6
SSM_GROUP = 16
SSM_GROUPS = D_MODEL // SSM_GROUP
SSM_STATE = 64
HEAD_DIM = 128
N_HEADS = D_MODEL // HEAD_DIM
N_KV_HEADS = 4
HEADS_PER_KV = N_HEADS // N_KV_HEADS
CMP_BLOCK = 32
CMP_STRIDE = 16
CMP_HIDDEN = 256
SEL_BLOCK = 64
N_SEL = 16
WINDOW = 512
N_BRANCH = 3
Q_WIDTH = N_HEADS * HEAD_DIM
KV_WIDTH = N_KV_HEADS * HEAD_DIM
NEG_INF = -1e30
FORCE_SCORE = 1e4

V7X_SUBLANES = 8
V7X_LANES = 128
V7X_VMEM_BYTES = 64 * 1024 * 1024
VMEM_LIMIT = V7X_VMEM_BYTES - 6 * 1024 * 1024

SCAN_SEGS = V7X_SUBLANES
SCAN_SUPER = 8
SCAN_COLS = SCAN_SUPER * SSM_STATE
SCAN_CH = SCAN_SUPER * SSM_GROUP
N_SUPER = SSM_GROUPS // SCAN_SUPER
SCAN_CHUNK = 32

Q_TILE = 128
Q_ROWS = Q_TILE * HEADS_PER_KV
SEL_TILE = 1024
CMP_BUCKET = 256
SEL_ROWS = 256
BIAS_BLOCKS = 128
BIAS_KEYS = BIAS_BLOCKS * SEL_BLOCK


def _params(*sem):
    return pltpu.CompilerParams(dimension_semantics=sem, vmem_limit_bytes=VMEM_LIMIT)


def _gelu(x):
    c = math.sqrt(2.0 / math.pi)
    return 0.5 * x * (1.0 + jnp.tanh(c * (x + 0.044715 * (x * x * x))))


def _dot(a, b):
    return jnp.dot(a, b, preferred_element_type=F32)


def _dot_nt(a, b):
    return lax.dot_general(a, b, (((1,), (1,)), ((), ())), preferred_element_type=F32)


def _rmsnorm_kernel(x_ref, g_ref, o_ref):
    x = x_ref[...]
    ms = jnp.mean(x * x, axis=-1, keepdims=True)
    o_ref[...] = ((x * lax.rsqrt(ms + NORM_EPS)) * g_ref[...]).astype(o_ref.dtype)


def rmsnorm(x, g, out_dtype, tm=256):
    m, d = x.shape
    return pl.pallas_call(
        _rmsnorm_kernel,
        grid=(m // tm,),
        in_specs=[pl.BlockSpec((tm, d), lambda i: (i, 0)),
                  pl.BlockSpec((1, d), lambda i: (0, 0))],
        out_specs=pl.BlockSpec((tm, d), lambda i: (i, 0)),
        out_shape=jax.ShapeDtypeStruct((m, d), out_dtype),
        compiler_params=_params("parallel"),
        name="rmsnorm",
    )(x, g.reshape(1, d).astype(F32))


def _mm_kernel(*refs, n_rhs, has_res, epilogue):
    lhs = refs[0][...]
    accs = [_dot(lhs, r[...].astype(lhs.dtype)) for r in refs[1:1 + n_rhs]]
    res = refs[1 + n_rhs][...] if has_res else None
    o_ref = refs[-1]
    o_ref[...] = epilogue(accs, res).astype(o_ref.dtype)


def matmul(lhs, rhs, rhs_col_offsets, n_out, epilogue, out_dtype, res=None, rhs_lead=(), tm=1024, tn=512,
           name="matmul"):
    m, k = lhs.shape
    tn = min(tn, n_out)
    tm = min(tm, m)
    n_rhs = len(rhs_col_offsets)
    lhs_mode = pl.Buffered(1) if rhs.dtype == F32 else None
    in_specs = [pl.BlockSpec((tm, k), lambda i, j: (i, 0), pipeline_mode=lhs_mode)]
    lead_block = (None,) * len(rhs_lead)
    for off in rhs_col_offsets:
        in_specs.append(pl.BlockSpec(lead_block + (k, tn),
                                     functools.partial(lambda i, j, ob: rhs_lead + (0, j + ob), ob=off // tn)))
    args = [lhs] + [rhs] * n_rhs
    if res is not None:
        in_specs.append(pl.BlockSpec((tm, tn), lambda i, j: (i, j)))
        args.append(res)
    return pl.pallas_call(
        functools.partial(_mm_kernel, n_rhs=n_rhs, has_res=res is not None, epilogue=epilogue),
        grid=(m // tm, n_out // tn),
        in_specs=in_specs,
        out_specs=pl.BlockSpec((tm, tn), lambda i, j: (i, j)),
        out_shape=jax.ShapeDtypeStruct((m, n_out), out_dtype),
        compiler_params=_params("parallel", "arbitrary"),
        name=name,
    )(*args)


def _ep_swiglu(accs, res):
    a, b = accs
    return (a * jax.nn.sigmoid(a)) * b


def _ep_half_residual(accs, res):
    return res + 0.5 * accs[0]


def _ep_residual(accs, res):
    return res + accs[0]


def _ep_glu_residual(accs, res):
    val, gate = accs
    return res + val * jax.nn.sigmoid(gate)


def _ep_scale_q(accs, res):
    return accs[0] * (HEAD_DIM ** -0.5 * math.log2(math.e))


def _ep_identity(accs, res):
    return accs[0]


def _ep_sigmoid(accs, res):
    return jax.nn.sigmoid(accs[0])


def ffn(x, g, w_in, w_out, lead=()):
    h = rmsnorm(x, g, BF16)
    act = matmul(h, w_in, (0, FFN_DIM), FFN_DIM, _ep_swiglu, BF16, rhs_lead=lead, name="ffn_in")
    return matmul(act, w_out, (0,), D_MODEL, _ep_half_residual, F32, res=x, rhs_lead=lead, name="ffn_out")


def _s5_discretise(lam_re, lam_im, log_step, b_re, b_im, seg_len):
    dt = jnp.exp(log_step.astype(F32))[:, None]
    lr, li = lam_re.astype(F32), lam_im.astype(F32)
    mag = jnp.exp(lr * dt)
    ang = li * dt
    ab_re, ab_im = mag * jnp.cos(ang), mag * jnp.sin(ang)
    den = lr * lr + li * li
    nr, ni = ab_re - 1.0, ab_im
    coef_re = (nr * lr + ni * li) / den
    coef_im = (ni * lr - nr * li) / den
    br, bi = b_re.astype(F32), b_im.astype(F32)
    bb_re = coef_re[..., None] * br - coef_im[..., None] * bi
    bb_im = coef_re[..., None] * bi + coef_im[..., None] * br
    mag_n = jnp.exp(seg_len * (lr * dt))
    ang_n = seg_len * ang
    return ab_re, ab_im, bb_re, bb_im, mag_n * jnp.cos(ang_n), mag_n * jnp.sin(ang_n)


def _s5_layout(ab_re, ab_im, bb_re, bb_im, c_re, c_im, an_re, an_im):
    eye = jnp.eye(SCAN_SUPER, dtype=F32)

    def in_proj(bb):
        w = bb.reshape(N_SUPER, SCAN_SUPER, SSM_STATE, SSM_GROUP).transpose(0, 1, 3, 2)
        w = jnp.einsum('sghp,gk->sghkp', w, eye)
        return w.reshape(N_SUPER, SCAN_CH, SCAN_COLS)

    def out_proj(c):
        w = c.reshape(N_SUPER, SCAN_SUPER, SSM_GROUP, SSM_STATE).transpose(0, 1, 3, 2)
        w = jnp.einsum('sgph,gk->sgpkh', w, eye)
        return w.reshape(N_SUPER, SCAN_COLS, SCAN_CH)

    w_b = jnp.concatenate([in_proj(bb_re), in_proj(bb_im)], axis=2).astype(BF16)
    w_c = jnp.concatenate([out_proj(c_re.astype(F32)), -out_proj(c_im.astype(F32))], axis=1).astype(BF16)

    def lanes(a):
        a = a.reshape(N_SUPER, 1, SCAN_COLS)
        return jnp.broadcast_to(a, (N_SUPER, SCAN_SEGS, SCAN_COLS))

    return w_b, w_c, lanes(ab_re), lanes(ab_im), an_re.reshape(N_SUPER, SCAN_COLS), an_im.reshape(N_SUPER, SCAN_COLS)


def _s5_chunk_pipeline(u_ref, wb_ref, wc_ref, d_ref, g_ref, bu_scr, h_scr, a_re, a_im, h_re, h_im, tb):
    n_chunks = tb // SCAN_CHUNK
    chunk_rows = SCAN_CHUNK * SCAN_SEGS

    def in_proj(k):
        rows = slice(k * chunk_rows, (k + 1) * chunk_rows)
        bu_scr[rows, :] = _dot(u_ref[rows, :].astype(BF16), wb_ref[0])

    def out_proj(k):
        rows = slice(k * chunk_rows, (k + 1) * chunk_rows)
        y = _dot(h_scr[rows, :].astype(BF16), wc_ref[0]) + d_ref[...] * u_ref[rows, :]
        g_ref[rows, :] = _gelu(y).astype(g_ref.dtype)

    in_proj(0)
    for k in range(n_chunks):
        if k + 1 < n_chunks:
            in_proj(k + 1)
        for t in range(k * SCAN_CHUNK, (k + 1) * SCAN_CHUNK):
            rows = slice(t * SCAN_SEGS, (t + 1) * SCAN_SEGS)
            bu = bu_scr[rows, :]
            h_re, h_im = ((a_re * h_re - a_im * h_im) + bu[:, :SCAN_COLS],
                          (a_re * h_im + a_im * h_re) + bu[:, SCAN_COLS:])
            if h_scr is not None:
                h_scr[rows, :SCAN_COLS] = h_re
                h_scr[rows, SCAN_COLS:] = h_im
        if h_scr is not None and k > 0:
            out_proj(k - 1)
    if h_scr is not None:
        out_proj(n_chunks - 1)
    return h_re, h_im


def _s5_state_kernel(u_ref, wb_ref, are_ref, aim_ref, hend_ref, bu_scr, st_scr, *, tb):
    tblk = pl.program_id(1)

    @pl.when(tblk == 0)
    def _():
        st_scr[...] = jnp.zeros_like(st_scr)

    st = st_scr[...]
    hr, hi = _s5_chunk_pipeline(u_ref, wb_ref, None, None, None, bu_scr, None, are_ref[0], aim_ref[0],
                                st[:, :SCAN_COLS], st[:, SCAN_COLS:], tb)
    st_scr[:, :SCAN_COLS] = hr
    st_scr[:, SCAN_COLS:] = hi

    @pl.when(tblk == pl.num_programs(1) - 1)
    def _():
        hend_ref[0] = st_scr[...]


def _s5_carry_kernel(hend_ref, an_re_ref, an_im_ref, s_ref):
    ar, ai = an_re_ref[...], an_im_ref[...]
    sr = jnp.zeros_like(ar)
    si = jnp.zeros_like(ar)
    for c in range(SCAN_SEGS):
        s_ref[c, :, :SCAN_COLS] = sr
        s_ref[c, :, SCAN_COLS:] = si
        hr = hend_ref[c, :, :SCAN_COLS]
        hi = hend_ref[c, :, SCAN_COLS:]
        sr, si = (ar * sr - ai * si) + hr, (ar * si + ai * sr) + hi


def _s5_output_kernel(u_ref, wb_ref, wc_ref, are_ref, aim_ref, s0_ref, d_ref, g_ref, bu_scr, h_scr, st_scr, *, tb):
    tblk = pl.program_id(1)

    @pl.when(tblk == 0)
    def _():
        st_scr[...] = s0_ref[0]

    st = st_scr[...]
    hr, hi = _s5_chunk_pipeline(u_ref, wb_ref, wc_ref, d_ref, g_ref, bu_scr, h_scr, are_ref[0], aim_ref[0],
                                st[:, :SCAN_COLS], st[:, SCAN_COLS:], tb)
    st_scr[:, :SCAN_COLS] = hr
    st_scr[:, SCAN_COLS:] = hi


def s5_mixer(xp, g_norm, lam_re, lam_im, log_step, b_re, b_im, c_re, c_im, d_skip, w_glu, tb=128):
    seq = xp.shape[0]
    seg_len = seq // SCAN_SEGS
    tb = min(tb, seg_len)
    assert tb % SCAN_CHUNK == 0
    rows = tb * SCAN_SEGS
    n_tblk = seg_len // tb
    u = rmsnorm(xp, g_norm, F32)
    disc = _s5_discretise(lam_re, lam_im, log_step, b_re, b_im, seg_len)
    w_b, w_c, a_re, a_im, an_re, an_im = _s5_layout(disc[0], disc[1], disc[2], disc[3], c_re, c_im, disc[4], disc[5])

    u_spec = pl.BlockSpec((rows, SCAN_CH), lambda s, t: (t, s))
    wb_spec = pl.BlockSpec((1, SCAN_CH, 2 * SCAN_COLS), lambda s, t: (s, 0, 0))
    a_spec = pl.BlockSpec((1, SCAN_SEGS, SCAN_COLS), lambda s, t: (s, 0, 0))
    st_spec = pl.BlockSpec((1, SCAN_SEGS, 2 * SCAN_COLS), lambda s, t: (s, 0, 0))

    h_end = pl.pallas_call(
        functools.partial(_s5_state_kernel, tb=tb),
        grid=(N_SUPER, n_tblk),
        in_specs=[u_spec, wb_spec, a_spec, a_spec],
        out_specs=st_spec,
        out_shape=jax.ShapeDtypeStruct((N_SUPER, SCAN_SEGS, 2 * SCAN_COLS), F32),
        scratch_shapes=[pltpu.VMEM((rows, 2 * SCAN_COLS), F32), pltpu.VMEM((SCAN_SEGS, 2 * SCAN_COLS), F32)],
        compiler_params=_params("parallel", "arbitrary"),
        name="s5_state",
    )(u, w_b, a_re, a_im)

    s0 = pl.pallas_call(
        _s5_carry_kernel,
        out_shape=jax.ShapeDtypeStruct((SCAN_SEGS, N_SUPER, 2 * SCAN_COLS), F32),
        name="s5_carry",
    )(h_end.transpose(1, 0, 2), an_re, an_im)
    s0 = s0.transpose(1, 0, 2)

    act = pl.pallas_call(
        functools.partial(_s5_output_kernel, tb=tb),
        grid=(N_SUPER, n_tblk),
        in_specs=[u_spec, wb_spec,
                  pl.BlockSpec((1, 2 * SCAN_COLS, SCAN_CH), lambda s, t: (s, 0, 0)),
                  a_spec, a_spec, st_spec,
                  pl.BlockSpec((1, SCAN_CH), lambda s, t: (0, s))],
        out_specs=pl.BlockSpec((rows, SCAN_CH), lambda s, t: (t, s)),
        out_shape=jax.ShapeDtypeStruct((seq, D_MODEL), BF16),
        scratch_shapes=[pltpu.VMEM((rows, 2 * SCAN_COLS), F32), pltpu.VMEM((rows, 2 * SCAN_COLS), F32),
                        pltpu.VMEM((SCAN_SEGS, 2 * SCAN_COLS), F32)],
        compiler_params=_params("parallel", "arbitrary"),
        name="s5_output",
    )(u, w_b, w_c, a_re, a_im, s0, d_skip.reshape(1, D_MODEL).astype(F32))

    return matmul(act, w_glu, (0, D_MODEL), D_MODEL, _ep_glu_residual, F32, res=xp, name="s5_glu")


def _to_segment_rows(x):
    seq, d = x.shape
    return x.reshape(SCAN_SEGS, seq // SCAN_SEGS, d).transpose(1, 0, 2).reshape(seq, d)


def _from_segment_rows(xp):
    seq, d = xp.shape
    return xp.reshape(seq // SCAN_SEGS, SCAN_SEGS, d).transpose(1, 0, 2).reshape(seq, d)


def _compress_kernel(x_ref, pe_ref, w1_ref, w2_ref, o_ref):
    x = x_ref[0]
    n_half = x.shape[0]
    half = (CMP_BLOCK // 2) * HEAD_DIM
    pe = pe_ref[0]
    xa = (x + pe[0:1, :]).astype(BF16)
    xb = (x + pe[1:2, :]).astype(BF16)
    top = _dot(xa, w1_ref[0, :half, :])
    bot = _dot(xb, w1_ref[0, half:, :])
    bot_next = pltpu.roll(bot, n_half - 1, 0)
    hid = _gelu(top + bot_next)
    out = _dot(hid.astype(BF16), w2_ref[0])
    row = lax.broadcasted_iota(jnp.int32, out.shape, 0)
    o_ref[0] = jnp.where(row < n_half - 1, out, 0.0).astype(o_ref.dtype)


def compress(kv_heads, pe, w1, w2):
    n_hd, n_half, width = kv_heads.shape
    return pl.pallas_call(
        _compress_kernel,
        grid=(n_hd,),
        in_specs=[pl.BlockSpec((1, n_half, width), lambda i: (i, 0, 0)),
                  pl.BlockSpec((1, 2, width), lambda i: (i // N_KV_HEADS, 0, 0)),
                  pl.BlockSpec((1, 2 * width, CMP_HIDDEN), lambda i: (i // N_KV_HEADS, 0, 0)),
                  pl.BlockSpec((1, CMP_HIDDEN, HEAD_DIM), lambda i: (i // N_KV_HEADS, 0, 0))],
        out_specs=pl.BlockSpec((1, n_half, HEAD_DIM), lambda i: (i, 0, 0)),
        out_shape=jax.ShapeDtypeStruct((n_hd, n_half, HEAD_DIM), BF16),
        compiler_params=_params("parallel"),
        name="nsa_compress",
    )(kv_heads, pe, w1, w2)


def _split3_dot(p, w):
    p1 = p.astype(BF16)
    r1 = p - p1.astype(F32)
    p2 = r1.astype(BF16)
    p3 = (r1 - p2.astype(F32)).astype(BF16)
    return _dot(p1, w) + _dot(p2, w) + _dot(p3, w)


def _head_rows(q_blk, g):
    base = g * HEADS_PER_KV * HEAD_DIM
    return jnp.concatenate(
        [q_blk[:, base + n * HEAD_DIM:base + (n + 1) * HEAD_DIM] for n in range(HEADS_PER_KV)], axis=0)


def _select_kernel(q_ref, kc_ref, ov_ref, oc_ref, bias_ref, imp_scr, *, n_sb, n_sbp, cmp_bucket):
    qb = pl.program_id(0)
    n_cmp = kc_ref.shape[1]
    t_rel = lax.broadcasted_iota(jnp.int32, (Q_ROWS, 1), 0) & (Q_TILE - 1)
    t_row = qb * Q_TILE + t_rel
    has_key = (t_row >= CMP_BLOCK - 1).astype(F32)

    def compressed_branch(n_cols):
        q_blk = q_ref[...]
        c_end = lax.broadcasted_iota(jnp.int32, (1, n_cols), 1) * CMP_STRIDE + (CMP_BLOCK - 1)
        visible = c_end <= t_row
        for g in range(N_KV_HEADS):
            qr = _head_rows(q_blk, g)
            s = jnp.where(visible, _dot_nt(qr, kc_ref[g, :n_cols, :]), NEG_INF)
            e = jnp.exp2(s - jnp.max(s, axis=1, keepdims=True))
            p_c = e * (has_key / jnp.sum(e, axis=1, keepdims=True))
            o_c = _dot(p_c.astype(BF16), kc_ref[N_KV_HEADS + g, :n_cols, :])
            for n in range(HEADS_PER_KV):
                col = (g * HEADS_PER_KV + n) * HEAD_DIM
                oc_ref[:, col:col + HEAD_DIM] = o_c[n * Q_TILE:(n + 1) * Q_TILE]
            p_sum = p_c[0:Q_TILE]
            for n in range(1, HEADS_PER_KV):
                p_sum = p_sum + p_c[n * Q_TILE:(n + 1) * Q_TILE]
            imp_scr[g * Q_TILE:(g + 1) * Q_TILE, :] = _split3_dot(p_sum, ov_ref[:n_cols, :])

    last_visible = (Q_TILE // CMP_STRIDE) * qb + (Q_TILE - CMP_BLOCK) // CMP_STRIDE
    for k in range(n_cmp // cmp_bucket):
        pl.when(last_visible // cmp_bucket == k)(functools.partial(compressed_branch, (k + 1) * cmp_bucket))
    imp = imp_scr[...]

    rows = N_KV_HEADS * Q_TILE
    t_q = qb * Q_TILE + (lax.broadcasted_iota(jnp.int32, (rows, 1), 0) & (Q_TILE - 1))
    blk = t_q >> int(math.log2(SEL_BLOCK))
    j_i = lax.broadcasted_iota(jnp.int32, (1, n_sb), 1)
    forced = jnp.where(j_i == 0, 1.0, jnp.where(j_i == blk, 1.0, jnp.where(j_i == blk - 1, 1.0, 0.0)))
    work = jnp.where(forced > 0.0, -3e38, jnp.where(j_i <= blk, imp, -1.0))
    picked = forced
    for _ in range(min(N_SEL, n_sb) - 3):
        hit = j_i == jnp.argmax(work, axis=1, keepdims=True).astype(jnp.int32)
        picked = jnp.where(hit, 1.0, picked)
        work = jnp.where(hit, -3e38, work)
    bias = jnp.where(picked > 0.0, jnp.where(j_i < 2 * qb, 0.0, NEG_INF), NEG_INF).astype(BF16)
    if n_sbp > n_sb:
        bias = jnp.concatenate([bias, jnp.full((rows, n_sbp - n_sb), NEG_INF, BF16)], axis=1)
    for g in range(N_KV_HEADS):
        bias_ref[:, g * n_sbp:(g + 1) * n_sbp] = bias[g * Q_TILE:(g + 1) * Q_TILE]


def _attend_kernel(q_ref, gate_ref, oc_ref, bias_ref, ks_ref, vs_ref, kw_ref, vw_ref, e_ref, o_ref,
                   qa_scr, acc_scr, m_scr, s_scr, ow_scr, *, n_bias, tiles_per_bias):
    qb = pl.program_id(1)
    qr = _head_rows(q_ref[...], 0)
    t_rel = lax.broadcasted_iota(jnp.int32, (Q_ROWS, 1), 0) & (Q_TILE - 1)
    t_row = qb * Q_TILE + t_rel
    ones_v = jnp.ones((max(WINDOW + Q_TILE, SEL_TILE), HEAD_DIM), BF16)

    bias = bias_ref[...]
    for a in range(n_bias):
        qa_scr[a, :, :HEAD_DIM] = qr
        for n in range(HEADS_PER_KV):
            qa_scr[a, n * Q_TILE:(n + 1) * Q_TILE, HEAD_DIM:] = bias[:, a * BIAS_BLOCKS:(a + 1) * BIAS_BLOCKS]

    k0 = pl.multiple_of(qb * Q_TILE, Q_TILE)
    s = _dot_nt(qr, ks_ref[pl.ds(k0, Q_TILE), :])
    k_rel = lax.broadcasted_iota(jnp.int32, (1, Q_TILE), 1)
    s = jnp.where(k_rel <= t_rel, s, NEG_INF)
    m0 = jnp.max(s, axis=1, keepdims=True)
    p = jnp.exp2(s - m0).astype(BF16)
    acc_scr[...] = _dot(p, jnp.concatenate([vs_ref[pl.ds(k0, Q_TILE), :], ones_v[:Q_TILE]], axis=1))
    m_scr[...] = jnp.broadcast_to(m0, (Q_ROWS, V7X_LANES))

    band = WINDOW + Q_TILE
    b0 = pl.multiple_of(jnp.maximum(qb * Q_TILE - WINDOW, 0), Q_TILE)
    s = _dot_nt(qr, kw_ref[pl.ds(b0, band), :])
    k_pos = b0 + lax.broadcasted_iota(jnp.int32, (1, band), 1)
    s = jnp.where(k_pos <= t_row, jnp.where(k_pos > t_row - WINDOW, s, NEG_INF), NEG_INF)
    p = jnp.exp2(s - jnp.max(s, axis=1, keepdims=True)).astype(BF16)
    acc_w = _dot(p, jnp.concatenate([vw_ref[pl.ds(b0, band), :], ones_v[:band]], axis=1))
    ow_scr[...] = acc_w[:, :HEAD_DIM] / acc_w[:, HEAD_DIM:]

    n_tiles = (qb * Q_TILE + SEL_TILE - 1) // SEL_TILE
    chunks = [pl.ds(c * SEL_ROWS, SEL_ROWS) for c in range(Q_ROWS // SEL_ROWS)]

    def keys_of(i):
        r0 = pl.multiple_of(i * SEL_TILE, SEL_TILE)
        a = i // tiles_per_bias
        e0 = pl.multiple_of((i - a * tiles_per_bias) * SEL_TILE, SEL_TILE)
        return a, jnp.concatenate([ks_ref[pl.ds(r0, SEL_TILE), :], e_ref[pl.ds(e0, SEL_TILE), :]], axis=1)

    a_first, k_first = keys_of(0)
    for rows in chunks:
        s_scr[rows, :] = _dot_nt(qa_scr[a_first, rows, :], k_first)

    def sel_step(i, has_next):
        r0 = pl.multiple_of(i * SEL_TILE, SEL_TILE)
        v_aug = jnp.concatenate([vs_ref[pl.ds(r0, SEL_TILE), :], ones_v[:SEL_TILE]], axis=1)
        if has_next:
            a_next, k_next = keys_of(i + 1)
        for rows in chunks:
            st = s_scr[rows, :]
            m_old = m_scr[rows, :]
            m_new = jnp.maximum(m_old, jnp.max(st, axis=1, keepdims=True))
            pt = jnp.exp2(st - jnp.concatenate([m_new] * (SEL_TILE // V7X_LANES), axis=1)).astype(BF16)
            alpha = jnp.exp2(m_old - m_new)
            acc_scr[rows, :] = jnp.concatenate([alpha, alpha], axis=1) * acc_scr[rows, :] + _dot(pt, v_aug)
            m_scr[rows, :] = m_new
            if has_next:
                s_scr[rows, :] = _dot_nt(qa_scr[a_next, rows, :], k_next)

    def sel_loop_body(i, carry):
        sel_step(i, True)
        return carry

    lax.fori_loop(0, n_tiles - 1, sel_loop_body, 0)

    @pl.when(n_tiles > 0)
    def _():
        sel_step(n_tiles - 1, False)

    gates = gate_ref[0]
    for n in range(HEADS_PER_KV):
        rows = pl.ds(n * Q_TILE, Q_TILE)
        cols = slice(n * HEAD_DIM, (n + 1) * HEAD_DIM)
        o_s = acc_scr[rows, :HEAD_DIM] / acc_scr[rows, HEAD_DIM:]
        o_n = (gates[0, :, n:n + 1] * oc_ref[:, cols] + gates[1, :, n:n + 1] * o_s
               + gates[2, :, n:n + 1] * ow_scr[rows, :])
        o_ref[:, cols] = o_n.astype(o_ref.dtype)


def nsa_attention(q, gates, kv_cmp, kv, overlap, onehot):
    seq = q.shape[0]
    n_cmp = kv_cmp.shape[1]
    n_sb = seq // SEL_BLOCK
    n_bias = pl.cdiv(n_sb, BIAS_BLOCKS)
    n_sbp = n_bias * BIAS_BLOCKS
    e_rows = onehot.shape[0]
    group_w = HEADS_PER_KV * HEAD_DIM

    o_cmp, bias = pl.pallas_call(
        functools.partial(_select_kernel, n_sb=n_sb, n_sbp=n_sbp, cmp_bucket=min(CMP_BUCKET, n_cmp)),
        grid=(seq // Q_TILE,),
        in_specs=[pl.BlockSpec((Q_TILE, Q_WIDTH), lambda i: (i, 0)),
                  pl.BlockSpec((2 * N_KV_HEADS, n_cmp, HEAD_DIM), lambda i: (0, 0, 0)),
                  pl.BlockSpec((n_cmp, n_sb), lambda i: (0, 0))],
        out_specs=[pl.BlockSpec((Q_TILE, Q_WIDTH), lambda i: (i, 0)),
                   pl.BlockSpec((Q_TILE, N_KV_HEADS * n_sbp), lambda i: (i, 0))],
        out_shape=[jax.ShapeDtypeStruct((seq, Q_WIDTH), F32),
                   jax.ShapeDtypeStruct((seq, N_KV_HEADS * n_sbp), BF16)],
        scratch_shapes=[pltpu.VMEM((N_KV_HEADS * Q_TILE, n_sb), F32)],
        compiler_params=_params("parallel"),
        name="nsa_select",
    )(q, kv_cmp, overlap)

    resident = pl.Buffered(1)
    kv_spec = lambda col: pl.BlockSpec((seq, HEAD_DIM), functools.partial(lambda g, i, c: (0, c + g), c=col),
                                       pipeline_mode=resident)
    return pl.pallas_call(
        functools.partial(_attend_kernel, n_bias=n_bias, tiles_per_bias=e_rows // SEL_TILE),
        grid=(N_KV_HEADS, seq // Q_TILE),
        in_specs=[pl.BlockSpec((Q_TILE, group_w), lambda g, i: (i, g)),
                  pl.BlockSpec((1, N_BRANCH, Q_TILE, HEADS_PER_KV), lambda g, i: (g, 0, i, 0)),
                  pl.BlockSpec((Q_TILE, group_w), lambda g, i: (i, g)),
                  pl.BlockSpec((Q_TILE, n_sbp), lambda g, i: (i, g)),
                  kv_spec(0), kv_spec(N_KV_HEADS), kv_spec(2 * N_KV_HEADS), kv_spec(3 * N_KV_HEADS),
                  pl.BlockSpec((e_rows, BIAS_BLOCKS), lambda g, i: (0, 0))],
        out_specs=pl.BlockSpec((Q_TILE, group_w), lambda g, i: (i, g)),
        out_shape=jax.ShapeDtypeStruct((seq, Q_WIDTH), BF16),
        scratch_shapes=[pltpu.VMEM((n_bias, Q_ROWS, 2 * HEAD_DIM), BF16),
                        pltpu.VMEM((Q_ROWS, 2 * HEAD_DIM), F32),
                        pltpu.VMEM((Q_ROWS, V7X_LANES), F32),
                        pltpu.VMEM((Q_ROWS, SEL_TILE), F32),
                        pltpu.VMEM((Q_ROWS, HEAD_DIM), F32)],
        compiler_params=_params("parallel", "arbitrary"),
        name="nsa_attend",
    )(q, gates, o_cmp, bias, kv, kv, kv, kv, onehot)


def nsa_mixer(x, g_norm, w_in, w_o, pe, w1, w2):
    seq = x.shape[0]
    h = rmsnorm(x, g_norm, BF16)
    q = matmul(h, w_in, (0,), Q_WIDTH, _ep_scale_q, BF16, name="nsa_q")
    c0 = Q_WIDTH
    kv_c = matmul(h, w_in, (c0,), 2 * KV_WIDTH, _ep_identity, F32, name="nsa_kv_cmp")
    c0 += 2 * KV_WIDTH
    kv = matmul(h, w_in, (c0,), 4 * KV_WIDTH, _ep_identity, BF16, name="nsa_kv")
    c0 += 4 * KV_WIDTH
    n_gate = N_BRANCH * N_HEADS
    gates = matmul(h, w_in, (c0,), V7X_LANES, _ep_sigmoid, F32, name="nsa_gates")
    gates = gates[:, :n_gate].reshape(seq, N_KV_HEADS, HEADS_PER_KV, N_BRANCH).transpose(1, 3, 0, 2)

    n_half = seq // CMP_STRIDE
    kv_heads = kv_c.reshape(seq, 2 * N_KV_HEADS, HEAD_DIM).transpose(1, 0, 2)
    kv_heads = kv_heads.reshape(2 * N_KV_HEADS, n_half, CMP_STRIDE * HEAD_DIM)
    kv_cmp = compress(kv_heads, pe, w1, w2)

    n_sb = seq // SEL_BLOCK
    c_start = jnp.arange(n_half) * CMP_STRIDE
    s_start = jnp.arange(n_sb) * SEL_BLOCK
    overlap = ((c_start[:, None] < s_start[None, :] + SEL_BLOCK)
               & (c_start[:, None] + CMP_BLOCK > s_start[None, :])
               & (jnp.arange(n_half)[:, None] < n_half - 1)).astype(BF16)
    e_rows = min(seq, BIAS_KEYS)
    onehot = ((jnp.arange(e_rows)[:, None] // SEL_BLOCK) == jnp.arange(BIAS_BLOCKS)[None, :]).astype(BF16)

    o = nsa_attention(q, gates, kv_cmp, kv, overlap, onehot)
    return matmul(o, w_o, (0,), D_MODEL, _ep_residual, F32, res=x, name="nsa_out")


def kernel(x, norm_g, final_norm_g, ffn_w_in, ffn_w_out, s5_lambda_re, s5_lambda_im, s5_log_step, s5_b_re, s5_b_im, s5_c_re, s5_c_im, s5_d, s5_w_glu, nsa_w_in, nsa_w_o, nsa_pe_k, nsa_w1_k, nsa_w2_k, nsa_pe_v, nsa_w1_v, nsa_w2_v):
    bsz, seq, d = x.shape
    assert bsz == 1 and d == D_MODEL and seq % (SCAN_SEGS * 128) == 0
    xs = x.reshape(seq, d).astype(F32)
    ffn_w_out = ffn_w_out.astype(BF16)
    s5_w_glu = s5_w_glu.astype(BF16)
    nsa_w_in = nsa_w_in.astype(BF16)
    nsa_w_o = nsa_w_o.astype(BF16)
    half = (CMP_BLOCK // 2) * HEAD_DIM
    pe = jnp.stack([nsa_pe_k, nsa_pe_v], axis=1).astype(F32).reshape(-1, 2, 2, half)
    w1 = jnp.stack([nsa_w1_k, nsa_w1_v], axis=1).astype(BF16)
    w2 = jnp.stack([nsa_w2_k, nsa_w2_v], axis=1).astype(BF16)

    for layer in range(DEPTH):
        j = layer // 2
        if layer % 2 == 0:
            xs = _to_segment_rows(xs)
        xs = ffn(xs, norm_g[layer, 0], ffn_w_in, ffn_w_out, (layer, 0))
        if layer % 2 == 0:
            xs = s5_mixer(xs, norm_g[layer, 1], s5_lambda_re[j], s5_lambda_im[j], s5_log_step[j],
                          s5_b_re[j], s5_b_im[j], s5_c_re[j], s5_c_im[j], s5_d[j], s5_w_glu[j])
        else:
            xs = nsa_mixer(xs, norm_g[layer, 1], nsa_w_in[j], nsa_w_o[j], pe[j], w1[j], w2[j])
        xs = ffn(xs, norm_g[layer, 2], ffn_w_in, ffn_w_out, (layer, 1))
        if layer % 2 == 0:
            xs = _from_segment_rows(xs)
    return rmsnorm(xs, final_norm_g, x.dtype).reshape(bsz, seq, d)
```

```python
import functools
import math

import jax
import jax.numpy as jnp
from jax import lax
from jax.experimental import pallas as pl
from jax.experimental.pallas import tpu as pltpu

F32 = jnp.float32
BF16 = jnp.bfloat16

D_MODEL = 4096
DEPTH = 4
FFN_DIM = 3 * D_MODEL // 2
NORM_EPS = 1e-6
SSM_GROUP = 16
SSM_GROUPS = D_MODEL // SSM_GROUP
SSM_STATE = 64
HEAD_DIM = 128
N_HEADS = D_MODEL // HEAD_DIM
N_KV_HEADS = 4
HEADS_PER_KV = N_HEADS // N_KV_HEADS
CMP_BLOCK = 32
CMP_STRIDE = 16
CMP_HIDDEN = 256
SEL_BLOCK = 64
N_SEL = 16
WINDOW = 512
N_BRANCH = 3
Q_WIDTH = N_HEADS * HEAD_DIM
KV_WIDTH = N_KV_HEADS * HEAD_DIM
NEG_INF = -1e30
FORCE_SCORE = 1e4

V7X_SUBLANES = 8
V7X_LANES = 128
V7X_VMEM_BYTES = 64 * 1024 * 1024
VMEM_LIMIT = V7X_VMEM_BYTES - 6 * 1024 * 1024

SCAN_SEGS = V7X_SUBLANES
SCAN_SUPER = 8
SCAN_COLS = SCAN_SUPER * SSM_STATE
SCAN_CH = SCAN_SUPER * SSM_GROUP
N_SUPER = SSM_GROUPS // SCAN_SUPER
SCAN_CHUNK = 32

Q_TILE = 128
Q_ROWS = Q_TILE * HEADS_PER_KV
SEL_TILE = 1024
CMP_BUCKET = 256
SEL_ROWS = 256
BIAS_BLOCKS = 128
BIAS_KEYS = BIAS_BLOCKS * SEL_BLOCK


def _params(*sem):
    return pltpu.CompilerParams(dimension_semantics=sem, vmem_limit_bytes=VMEM_LIMIT)


def _gelu(x):
    c = math.sqrt(2.0 / math.pi)
    return 0.5 * x * (1.0 + jnp.tanh(c * (x + 0.044715 * (x * x * x))))


def _dot(a, b):
    return jnp.dot(a, b, preferred_element_type=F32)


def _dot_nt(a, b):
    return lax.dot_general(a, b, (((1,), (1,)), ((), ())), preferred_element_type=F32)


def _rmsnorm_kernel(x_ref, g_ref, o_ref):
    x = x_ref[...]
    ms = jnp.mean(x * x, axis=-1, keepdims=True)
    o_ref[...] = ((x * lax.rsqrt(ms + NORM_EPS)) * g_ref[...]).astype(o_ref.dtype)


def rmsnorm(x, g, out_dtype, tm=256):
    m, d = x.shape
    return pl.pallas_call(
        _rmsnorm_kernel,
        grid=(m // tm,),
        in_specs=[pl.BlockSpec((tm, d), lambda i: (i, 0)),
                  pl.BlockSpec((1, d), lambda i: (0, 0))],
        out_specs=pl.BlockSpec((tm, d), lambda i: (i, 0)),
        out_shape=jax.ShapeDtypeStruct((m, d), out_dtype),
        compiler_params=_params("parallel"),
        name="rmsnorm",
    )(x, g.reshape(1, d).astype(F32))


def _mm_kernel(*refs, n_rhs, has_res, has_norm_in, has_norm_out, epilogue):
    refs = iter(refs)
    lhs = next(refs)[...]
    rhs_refs = [next(refs) for _ in range(n_rhs)]
    res = next(refs)[...] if has_res else None
    ssq_in_ref = next(refs) if has_norm_in else None
    gain_ref = next(refs) if has_norm_out else None
    o_ref = next(refs)
    accs = [_dot(lhs, r[...].astype(lhs.dtype)) for r in rhs_refs]
    if has_norm_in:
        inv_rms = lax.rsqrt(ssq_in_ref[:, :1] * (1.0 / lhs.shape[1]) + NORM_EPS)
        accs = [a * inv_rms for a in accs]
    out = epilogue(accs, res)
    o_ref[...] = out.astype(o_ref.dtype)
    if has_norm_out:
        xg_ref, ssq_ref = next(refs), next(refs)
        xg_ref[...] = (out * gain_ref[...]).astype(xg_ref.dtype)
        part = jnp.broadcast_to(jnp.sum(out * out, axis=1, keepdims=True), ssq_ref.shape)
        first = pl.program_id(1) == 0

        @pl.when(first)
        def _():
            ssq_ref[...] = part

        @pl.when(jnp.logical_not(first))
        def _():
            ssq_ref[...] += part


def matmul(lhs, rhs, rhs_col_offsets, n_out, epilogue, out_dtype, res=None, rhs_lead=(), norm_in=None,
           norm_out_gain=None, tm=1024, tn=512, name="matmul"):
    m, k = lhs.shape
    tn = min(tn, n_out)
    tm = min(tm, m)
    n_rhs = len(rhs_col_offsets)
    lhs_mode = pl.Buffered(1) if rhs.dtype == F32 else None
    in_specs = [pl.BlockSpec((tm, k), lambda i, j: (i, 0), pipeline_mode=lhs_mode)]
    lead_block = (None,) * len(rhs_lead)
    for off in rhs_col_offsets:
        in_specs.append(pl.BlockSpec(lead_block + (k, tn),
                                     functools.partial(lambda i, j, ob: rhs_lead + (0, j + ob), ob=off // tn)))
    args = [lhs] + [rhs] * n_rhs
    tile_spec = pl.BlockSpec((tm, tn), lambda i, j: (i, j))
    row_spec = pl.BlockSpec((tm, V7X_LANES), lambda i, j: (i, 0))
    if res is not None:
        in_specs.append(tile_spec)
        args.append(res)
    if norm_in is not None:
        in_specs.append(row_spec)
        args.append(norm_in)
    out_specs, out_shape = tile_spec, jax.ShapeDtypeStruct((m, n_out), out_dtype)
    if norm_out_gain is not None:
        in_specs.append(pl.BlockSpec((1, tn), lambda i, j: (0, j)))
        args.append(norm_out_gain.reshape(1, n_out).astype(F32))
        out_specs = [tile_spec, tile_spec, row_spec]
        out_shape = [out_shape, jax.ShapeDtypeStruct((m, n_out), BF16), jax.ShapeDtypeStruct((m, V7X_LANES), F32)]
    return pl.pallas_call(
        functools.partial(_mm_kernel, n_rhs=n_rhs, has_res=res is not None, has_norm_in=norm_in is not None,
                          has_norm_out=norm_out_gain is not None, epilogue=epilogue),
        grid=(m // tm, n_out // tn),
        in_specs=in_specs,
        out_specs=out_specs,
        out_shape=out_shape,
        compiler_params=_params("parallel", "arbitrary"),
        name=name,
    )(*args)


def _ep_swiglu(accs, res):
    a, b = accs
    return (a * jax.nn.sigmoid(a)) * b


def _ep_half_residual(accs, res):
    return res + 0.5 * accs[0]


def _ep_residual(accs, res):
    return res + accs[0]


def _ep_glu_residual(accs, res):
    val, gate = accs
    return res + val * jax.nn.sigmoid(gate)


def _ep_scale_q(accs, res):
    return accs[0] * (HEAD_DIM ** -0.5 * math.log2(math.e))


def _ep_identity(accs, res):
    return accs[0]


def _ep_sigmoid(accs, res):
    return jax.nn.sigmoid(accs[0])


def _with_norm(result, next_gain):
    return (result[0], (result[1], result[2])) if next_gain is not None else (result, None)


def ffn(x, g, w_in, w_out, lead=(), pre=None, next_gain=None):
    if pre is None:
        act = matmul(rmsnorm(x, g, BF16), w_in, (0, FFN_DIM), FFN_DIM, _ep_swiglu, BF16, rhs_lead=lead, name="ffn_in")
    else:
        act = matmul(pre[0], w_in, (0, FFN_DIM), FFN_DIM, _ep_swiglu, BF16, rhs_lead=lead, norm_in=pre[1],
                     name="ffn_in")
    out = matmul(act, w_out, (0,), D_MODEL, _ep_half_residual, F32, res=x, rhs_lead=lead, norm_out_gain=next_gain,
                 name="ffn_out")
    return _with_norm(out, next_gain)


def _s5_discretise(lam_re, lam_im, log_step, b_re, b_im, seg_len):
    dt = jnp.exp(log_step.astype(F32))[:, None]
    lr, li = lam_re.astype(F32), lam_im.astype(F32)
    mag = jnp.exp(lr * dt)
    ang = li * dt
    ab_re, ab_im = mag * jnp.cos(ang), mag * jnp.sin(ang)
    den = lr * lr + li * li
    nr, ni = ab_re - 1.0, ab_im
    coef_re = (nr * lr + ni * li) / den
    coef_im = (ni * lr - nr * li) / den
    br, bi = b_re.astype(F32), b_im.astype(F32)
    bb_re = coef_re[..., None] * br - coef_im[..., None] * bi
    bb_im = coef_re[..., None] * bi + coef_im[..., None] * br
    mag_n = jnp.exp(seg_len * (lr * dt))
    ang_n = seg_len * ang
    return ab_re, ab_im, bb_re, bb_im, mag_n * jnp.cos(ang_n), mag_n * jnp.sin(ang_n)


def _s5_layout(ab_re, ab_im, bb_re, bb_im, c_re, c_im, an_re, an_im):
    eye = jnp.eye(SCAN_SUPER, dtype=F32)

    def in_proj(bb):
        w = bb.reshape(N_SUPER, SCAN_SUPER, SSM_STATE, SSM_GROUP).transpose(0, 1, 3, 2)
        w = jnp.einsum('sghp,gk->sghkp', w, eye)
        return w.reshape(N_SUPER, SCAN_CH, SCAN_COLS)

    def out_proj(c):
        w = c.reshape(N_SUPER, SCAN_SUPER, SSM_GROUP, SSM_STATE).transpose(0, 1, 3, 2)
        w = jnp.einsum('sgph,gk->sgpkh', w, eye)
        return w.reshape(N_SUPER, SCAN_COLS, SCAN_CH)

    w_b = jnp.concatenate([in_proj(bb_re), in_proj(bb_im)], axis=2).astype(BF16)
    w_c = jnp.concatenate([out_proj(c_re.astype(F32)), -out_proj(c_im.astype(F32))], axis=1).astype(BF16)

    def lanes(a):
        a = a.reshape(N_SUPER, 1, SCAN_COLS)
        return jnp.broadcast_to(a, (N_SUPER, SCAN_SEGS, SCAN_COLS))

    return w_b, w_c, lanes(ab_re), lanes(ab_im), an_re.reshape(N_SUPER, SCAN_COLS), an_im.reshape(N_SUPER, SCAN_COLS)


def _s5_chunk_pipeline(u_ref, wb_ref, wc_ref, d_ref, g_ref, bu_scr, h_scr, a_re, a_im, h_re, h_im, tb):
    n_chunks = tb // SCAN_CHUNK
    chunk_rows = SCAN_CHUNK * SCAN_SEGS

    def in_proj(k):
        rows = slice(k * chunk_rows, (k + 1) * chunk_rows)
        bu_scr[rows, :] = _dot(u_ref[rows, :].astype(BF16), wb_ref[0])

    def out_proj(k):
        rows = slice(k * chunk_rows, (k + 1) * chunk_rows)
        y = _dot(h_scr[rows, :].astype(BF16), wc_ref[0]) + d_ref[...] * u_ref[rows, :]
        g_ref[rows, :] = _gelu(y).astype(g_ref.dtype)

    in_proj(0)
    for k in range(n_chunks):
        if k + 1 < n_chunks:
            in_proj(k + 1)
        for t in range(k * SCAN_CHUNK, (k + 1) * SCAN_CHUNK):
            rows = slice(t * SCAN_SEGS, (t + 1) * SCAN_SEGS)
            bu = bu_scr[rows, :]
            h_re, h_im = ((a_re * h_re - a_im * h_im) + bu[:, :SCAN_COLS],
                          (a_re * h_im + a_im * h_re) + bu[:, SCAN_COLS:])
            if h_scr is not None:
                h_scr[rows, :SCAN_COLS] = h_re
                h_scr[rows, SCAN_COLS:] = h_im
        if h_scr is not None and k > 0:
            out_proj(k - 1)
    if h_scr is not None:
        out_proj(n_chunks - 1)
    return h_re, h_im


def _s5_state_kernel(u_ref, wb_ref, are_ref, aim_ref, hend_ref, bu_scr, st_scr, *, tb):
    tblk = pl.program_id(1)

    @pl.when(tblk == 0)
    def _():
        st_scr[...] = jnp.zeros_like(st_scr)

    st = st_scr[...]
    hr, hi = _s5_chunk_pipeline(u_ref, wb_ref, None, None, None, bu_scr, None, are_ref[0], aim_ref[0],
                                st[:, :SCAN_COLS], st[:, SCAN_COLS:], tb)
    st_scr[:, :SCAN_COLS] = hr
    st_scr[:, SCAN_COLS:] = hi

    @pl.when(tblk == pl.num_programs(1) - 1)
    def _():
        hend_ref[0] = st_scr[...]


def _s5_carry_kernel(hend_ref, an_re_ref, an_im_ref, s_ref):
    ar, ai = an_re_ref[...], an_im_ref[...]
    sr = jnp.zeros_like(ar)
    si = jnp.zeros_like(ar)
    for c in range(SCAN_SEGS):
        s_ref[c, :, :SCAN_COLS] = sr
        s_ref[c, :, SCAN_COLS:] = si
        hr = hend_ref[c, :, :SCAN_COLS]
        hi = hend_ref[c, :, SCAN_COLS:]
        sr, si = (ar * sr - ai * si) + hr, (ar * si + ai * sr) + hi


def _s5_output_kernel(u_ref, wb_ref, wc_ref, are_ref, aim_ref, s0_ref, d_ref, g_ref, bu_scr, h_scr, st_scr, *, tb):
    tblk = pl.program_id(1)

    @pl.when(tblk == 0)
    def _():
        st_scr[...] = s0_ref[0]

    st = st_scr[...]
    hr, hi = _s5_chunk_pipeline(u_ref, wb_ref, wc_ref, d_ref, g_ref, bu_scr, h_scr, are_ref[0], aim_ref[0],
                                st[:, :SCAN_COLS], st[:, SCAN_COLS:], tb)
    st_scr[:, :SCAN_COLS] = hr
    st_scr[:, SCAN_COLS:] = hi


def s5_mixer(x, g_norm, lam_re, lam_im, log_step, b_re, b_im, c_re, c_im, d_skip, w_glu, next_gain=None, tb=128):
    seq = x.shape[0]
    seg_len = seq // SCAN_SEGS
    tb = min(tb, seg_len)
    assert tb % SCAN_CHUNK == 0
    rows = tb * SCAN_SEGS
    n_tblk = seg_len // tb
    u = _to_segment_rows(rmsnorm(x, g_norm, F32))
    disc = _s5_discretise(lam_re, lam_im, log_step, b_re, b_im, seg_len)
    w_b, w_c, a_re, a_im, an_re, an_im = _s5_layout(disc[0], disc[1], disc[2], disc[3], c_re, c_im, disc[4], disc[5])

    u_spec = pl.BlockSpec((rows, SCAN_CH), lambda s, t: (t, s))
    wb_spec = pl.BlockSpec((1, SCAN_CH, 2 * SCAN_COLS), lambda s, t: (s, 0, 0))
    a_spec = pl.BlockSpec((1, SCAN_SEGS, SCAN_COLS), lambda s, t: (s, 0, 0))
    st_spec = pl.BlockSpec((1, SCAN_SEGS, 2 * SCAN_COLS), lambda s, t: (s, 0, 0))

    h_end = pl.pallas_call(
        functools.partial(_s5_state_kernel, tb=tb),
        grid=(N_SUPER, n_tblk),
        in_specs=[u_spec, wb_spec, a_spec, a_spec],
        out_specs=st_spec,
        out_shape=jax.ShapeDtypeStruct((N_SUPER, SCAN_SEGS, 2 * SCAN_COLS), F32),
        scratch_shapes=[pltpu.VMEM((rows, 2 * SCAN_COLS), F32), pltpu.VMEM((SCAN_SEGS, 2 * SCAN_COLS), F32)],
        compiler_params=_params("parallel", "arbitrary"),
        name="s5_state",
    )(u, w_b, a_re, a_im)

    s0 = pl.pallas_call(
        _s5_carry_kernel,
        out_shape=jax.ShapeDtypeStruct((SCAN_SEGS, N_SUPER, 2 * SCAN_COLS), F32),
        name="s5_carry",
    )(h_end.transpose(1, 0, 2), an_re, an_im)
    s0 = s0.transpose(1, 0, 2)

    act = pl.pallas_call(
        functools.partial(_s5_output_kernel, tb=tb),
        grid=(N_SUPER, n_tblk),
        in_specs=[u_spec, wb_spec,
                  pl.BlockSpec((1, 2 * SCAN_COLS, SCAN_CH), lambda s, t: (s, 0, 0)),
                  a_spec, a_spec, st_spec,
                  pl.BlockSpec((1, SCAN_CH), lambda s, t: (0, s))],
        out_specs=pl.BlockSpec((rows, SCAN_CH), lambda s, t: (t, s)),
        out_shape=jax.ShapeDtypeStruct((seq, D_MODEL), BF16),
        scratch_shapes=[pltpu.VMEM((rows, 2 * SCAN_COLS), F32), pltpu.VMEM((rows, 2 * SCAN_COLS), F32),
                        pltpu.VMEM((SCAN_SEGS, 2 * SCAN_COLS), F32)],
        compiler_params=_params("parallel", "arbitrary"),
        name="s5_output",
    )(u, w_b, w_c, a_re, a_im, s0, d_skip.reshape(1, D_MODEL).astype(F32))

    out = matmul(_from_segment_rows(act), w_glu, (0, D_MODEL), D_MODEL, _ep_glu_residual, F32, res=x,
                 norm_out_gain=next_gain, name="s5_glu")
    return _with_norm(out, next_gain)


def _to_segment_rows(x):
    seq, d = x.shape
    return x.reshape(SCAN_SEGS, seq // SCAN_SEGS, d).transpose(1, 0, 2).reshape(seq, d)


def _from_segment_rows(xp):
    seq, d = xp.shape
    return xp.reshape(seq // SCAN_SEGS, SCAN_SEGS, d).transpose(1, 0, 2).reshape(seq, d)


def _compress_kernel(x_ref, pe_ref, w1_ref, w2_ref, o_ref):
    x = x_ref[0]
    n_half = x.shape[0]
    half = (CMP_BLOCK // 2) * HEAD_DIM
    pe = pe_ref[0]
    xa = (x + pe[0:1, :]).astype(BF16)
    xb = (x + pe[1:2, :]).astype(BF16)
    top = _dot(xa, w1_ref[0, :half, :])
    bot = _dot(xb, w1_ref[0, half:, :])
    bot_next = pltpu.roll(bot, n_half - 1, 0)
    hid = _gelu(top + bot_next)
    out = _dot(hid.astype(BF16), w2_ref[0])
    row = lax.broadcasted_iota(jnp.int32, out.shape, 0)
    o_ref[0] = jnp.where(row < n_half - 1, out, 0.0).astype(o_ref.dtype)


def compress(kv_heads, pe, w1, w2):
    n_hd, n_half, width = kv_heads.shape
    return pl.pallas_call(
        _compress_kernel,
        grid=(n_hd,),
        in_specs=[pl.BlockSpec((1, n_half, width), lambda i: (i, 0, 0)),
                  pl.BlockSpec((1, 2, width), lambda i: (i // N_KV_HEADS, 0, 0)),
                  pl.BlockSpec((1, 2 * width, CMP_HIDDEN), lambda i: (i // N_KV_HEADS, 0, 0)),
                  pl.BlockSpec((1, CMP_HIDDEN, HEAD_DIM), lambda i: (i // N_KV_HEADS, 0, 0))],
        out_specs=pl.BlockSpec((1, n_half, HEAD_DIM), lambda i: (i, 0, 0)),
        out_shape=jax.ShapeDtypeStruct((n_hd, n_half, HEAD_DIM), BF16),
        compiler_params=_params("parallel"),
        name="nsa_compress",
    )(kv_heads, pe, w1, w2)


def _split3_dot(p, w):
    p1 = p.astype(BF16)
    r1 = p - p1.astype(F32)
    p2 = r1.astype(BF16)
    p3 = (r1 - p2.astype(F32)).astype(BF16)
    return _dot(p1, w) + _dot(p2, w) + _dot(p3, w)


def _head_rows(q_blk, g):
    base = g * HEADS_PER_KV * HEAD_DIM
    return jnp.concatenate(
        [q_blk[:, base + n * HEAD_DIM:base + (n + 1) * HEAD_DIM] for n in range(HEADS_PER_KV)], axis=0)


def _select_kernel(q_ref, kc_ref, ov_ref, oc_ref, bias_ref, imp_scr, *, n_sb, n_sbp, cmp_bucket):
    qb = pl.program_id(0)
    n_cmp = kc_ref.shape[1]
    t_rel = lax.broadcasted_iota(jnp.int32, (Q_ROWS, 1), 0) & (Q_TILE - 1)
    t_row = qb * Q_TILE + t_rel
    has_key = (t_row >= CMP_BLOCK - 1).astype(F32)

    def compressed_branch(n_cols):
        q_blk = q_ref[...]
        c_end = lax.broadcasted_iota(jnp.int32, (1, n_cols), 1) * CMP_STRIDE + (CMP_BLOCK - 1)
        visible = c_end <= t_row
        for g in range(N_KV_HEADS):
            qr = _head_rows(q_blk, g)
            s = jnp.where(visible, _dot_nt(qr, kc_ref[g, :n_cols, :]), NEG_INF)
            e = jnp.exp2(s - jnp.max(s, axis=1, keepdims=True))
            p_c = e * (has_key / jnp.sum(e, axis=1, keepdims=True))
            o_c = _dot(p_c.astype(BF16), kc_ref[N_KV_HEADS + g, :n_cols, :])
            for n in range(HEADS_PER_KV):
                col = (g * HEADS_PER_KV + n) * HEAD_DIM
                oc_ref[:, col:col + HEAD_DIM] = o_c[n * Q_TILE:(n + 1) * Q_TILE]
            p_sum = p_c[0:Q_TILE]
            for n in range(1, HEADS_PER_KV):
                p_sum = p_sum + p_c[n * Q_TILE:(n + 1) * Q_TILE]
            imp_scr[g * Q_TILE:(g + 1) * Q_TILE, :] = _split3_dot(p_sum, ov_ref[:n_cols, :])

    last_visible = (Q_TILE // CMP_STRIDE) * qb + (Q_TILE - CMP_BLOCK) // CMP_STRIDE
    for k in range(n_cmp // cmp_bucket):
        pl.when(last_visible // cmp_bucket == k)(functools.partial(compressed_branch, (k + 1) * cmp_bucket))
    imp = imp_scr[...]

    rows = N_KV_HEADS * Q_TILE
    t_q = qb * Q_TILE + (lax.broadcasted_iota(jnp.int32, (rows, 1), 0) & (Q_TILE - 1))
    blk = t_q >> int(math.log2(SEL_BLOCK))
    j_i = lax.broadcasted_iota(jnp.int32, (1, n_sb), 1)
    forced = jnp.where(j_i == 0, 1.0, jnp.where(j_i == blk, 1.0, jnp.where(j_i == blk - 1, 1.0, 0.0)))
    work = jnp.where(forced > 0.0, -3e38, jnp.where(j_i <= blk, imp, -1.0))
    picked = forced
    for _ in range(min(N_SEL, n_sb) - 3):
        hit = j_i == jnp.argmax(work, axis=1, keepdims=True).astype(jnp.int32)
        picked = jnp.where(hit, 1.0, picked)
        work = jnp.where(hit, -3e38, work)
    bias = jnp.where(picked > 0.0, jnp.where(j_i < 2 * qb, 0.0, NEG_INF), NEG_INF).astype(BF16)
    if n_sbp > n_sb:
        bias = jnp.concatenate([bias, jnp.full((rows, n_sbp - n_sb), NEG_INF, BF16)], axis=1)
    for g in range(N_KV_HEADS):
        bias_ref[:, g * n_sbp:(g + 1) * n_sbp] = bias[g * Q_TILE:(g + 1) * Q_TILE]


def _attend_kernel(q_ref, gate_ref, oc_ref, bias_ref, ks_ref, vs_ref, kw_ref, vw_ref, e_ref, o_ref,
                   qa_scr, acc_scr, m_scr, s_scr, ow_scr, *, n_bias, tiles_per_bias):
    qb = pl.program_id(1)
    qr = _head_rows(q_ref[...], 0)
    t_rel = lax.broadcasted_iota(jnp.int32, (Q_ROWS, 1), 0) & (Q_TILE - 1)
    t_row = qb * Q_TILE + t_rel
    ones_v = jnp.ones((max(WINDOW + Q_TILE, SEL_TILE), HEAD_DIM), BF16)

    bias = bias_ref[...]
    for a in range(n_bias):
        qa_scr[a, :, :HEAD_DIM] = qr
        for n in range(HEADS_PER_KV):
            qa_scr[a, n * Q_TILE:(n + 1) * Q_TILE, HEAD_DIM:] = bias[:, a * BIAS_BLOCKS:(a + 1) * BIAS_BLOCKS]

    k0 = pl.multiple_of(qb * Q_TILE, Q_TILE)
    s = _dot_nt(qr, ks_ref[pl.ds(k0, Q_TILE), :])
    k_rel = lax.broadcasted_iota(jnp.int32, (1, Q_TILE), 1)
    s = jnp.where(k_rel <= t_rel, s, NEG_INF)
    m0 = jnp.max(s, axis=1, keepdims=True)
    p = jnp.exp2(s - m0).astype(BF16)
    acc_scr[...] = _dot(p, jnp.concatenate([vs_ref[pl.ds(k0, Q_TILE), :], ones_v[:Q_TILE]], axis=1))
    m_scr[...] = jnp.broadcast_to(m0, (Q_ROWS, V7X_LANES))

    band = WINDOW + Q_TILE
    b0 = pl.multiple_of(jnp.maximum(qb * Q_TILE - WINDOW, 0), Q_TILE)
    s = _dot_nt(qr, kw_ref[pl.ds(b0, band), :])
    k_pos = b0 + lax.broadcasted_iota(jnp.int32, (1, band), 1)
    s = jnp.where(k_pos <= t_row, jnp.where(k_pos > t_row - WINDOW, s, NEG_INF), NEG_INF)
    p = jnp.exp2(s - jnp.max(s, axis=1, keepdims=True)).astype(BF16)
    acc_w = _dot(p, jnp.concatenate([vw_ref[pl.ds(b0, band), :], ones_v[:band]], axis=1))
    ow_scr[...] = acc_w[:, :HEAD_DIM] / acc_w[:, HEAD_DIM:]

    n_tiles = (qb * Q_TILE + SEL_TILE - 1) // SEL_TILE
    chunks = [pl.ds(c * SEL_ROWS, SEL_ROWS) for c in range(Q_ROWS // SEL_ROWS)]

    def keys_of(i):
        r0 = pl.multiple_of(i * SEL_TILE, SEL_TILE)
        a = i // tiles_per_bias
        e0 = pl.multiple_of((i - a * tiles_per_bias) * SEL_TILE, SEL_TILE)
        return a, jnp.concatenate([ks_ref[pl.ds(r0, SEL_TILE), :], e_ref[pl.ds(e0, SEL_TILE), :]], axis=1)

    a_first, k_first = keys_of(0)
    for rows in chunks:
        s_scr[rows, :] = _dot_nt(qa_scr[a_first, rows, :], k_first)

    def sel_step(i, has_next):
        r0 = pl.multiple_of(i * SEL_TILE, SEL_TILE)
        v_aug = jnp.concatenate([vs_ref[pl.ds(r0, SEL_TILE), :], ones_v[:SEL_TILE]], axis=1)
        if has_next:
            a_next, k_next = keys_of(i + 1)
        for rows in chunks:
            st = s_scr[rows, :]
            m_old = m_scr[rows, :]
            m_new = jnp.maximum(m_old, jnp.max(st, axis=1, keepdims=True))
            pt = jnp.exp2(st - jnp.concatenate([m_new] * (SEL_TILE // V7X_LANES), axis=1)).astype(BF16)
            alpha = jnp.exp2(m_old - m_new)
            acc_scr[rows, :] = jnp.concatenate([alpha, alpha], axis=1) * acc_scr[rows, :] + _dot(pt, v_aug)
            m_scr[rows, :] = m_new
            if has_next:
                s_scr[rows, :] = _dot_nt(qa_scr[a_next, rows, :], k_next)

    def sel_loop_body(i, carry):
        sel_step(i, True)
        return carry

    lax.fori_loop(0, n_tiles - 1, sel_loop_body, 0)

    @pl.when(n_tiles > 0)
    def _():
        sel_step(n_tiles - 1, False)

    gates = gate_ref[0]
    for n in range(HEADS_PER_KV):
        rows = pl.ds(n * Q_TILE, Q_TILE)
        cols = slice(n * HEAD_DIM, (n + 1) * HEAD_DIM)
        o_s = acc_scr[rows, :HEAD_DIM] / acc_scr[rows, HEAD_DIM:]
        o_n = (gates[0, :, n:n + 1] * oc_ref[:, cols] + gates[1, :, n:n + 1] * o_s
               + gates[2, :, n:n + 1] * ow_scr[rows, :])
        o_ref[:, cols] = o_n.astype(o_ref.dtype)


def nsa_attention(q, gates, kv_cmp, kv, overlap, onehot):
    seq = q.shape[0]
    n_cmp = kv_cmp.shape[1]
    n_sb = seq // SEL_BLOCK
    n_bias = pl.cdiv(n_sb, BIAS_BLOCKS)
    n_sbp = n_bias * BIAS_BLOCKS
    e_rows = onehot.shape[0]
    group_w = HEADS_PER_KV * HEAD_DIM

    o_cmp, bias = pl.pallas_call(
        functools.partial(_select_kernel, n_sb=n_sb, n_sbp=n_sbp, cmp_bucket=min(CMP_BUCKET, n_cmp)),
        grid=(seq // Q_TILE,),
        in_specs=[pl.BlockSpec((Q_TILE, Q_WIDTH), lambda i: (i, 0)),
                  pl.BlockSpec((2 * N_KV_HEADS, n_cmp, HEAD_DIM), lambda i: (0, 0, 0)),
                  pl.BlockSpec((n_cmp, n_sb), lambda i: (0, 0))],
        out_specs=[pl.BlockSpec((Q_TILE, Q_WIDTH), lambda i: (i, 0)),
                   pl.BlockSpec((Q_TILE, N_KV_HEADS * n_sbp), lambda i: (i, 0))],
        out_shape=[jax.ShapeDtypeStruct((seq, Q_WIDTH), F32),
                   jax.ShapeDtypeStruct((seq, N_KV_HEADS * n_sbp), BF16)],
        scratch_shapes=[pltpu.VMEM((N_KV_HEADS * Q_TILE, n_sb), F32)],
        compiler_params=_params("parallel"),
        name="nsa_select",
    )(q, kv_cmp, overlap)

    resident = pl.Buffered(1)
    kv_spec = lambda col: pl.BlockSpec((seq, HEAD_DIM), functools.partial(lambda g, i, c: (0, c + g), c=col),
                                       pipeline_mode=resident)
    return pl.pallas_call(
        functools.partial(_attend_kernel, n_bias=n_bias, tiles_per_bias=e_rows // SEL_TILE),
        grid=(N_KV_HEADS, seq // Q_TILE),
        in_specs=[pl.BlockSpec((Q_TILE, group_w), lambda g, i: (i, g)),
                  pl.BlockSpec((1, N_BRANCH, Q_TILE, HEADS_PER_KV), lambda g, i: (g, 0, i, 0)),
                  pl.BlockSpec((Q_TILE, group_w), lambda g, i: (i, g)),
                  pl.BlockSpec((Q_TILE, n_sbp), lambda g, i: (i, g)),
                  kv_spec(0), kv_spec(N_KV_HEADS), kv_spec(2 * N_KV_HEADS), kv_spec(3 * N_KV_HEADS),
                  pl.BlockSpec((e_rows, BIAS_BLOCKS), lambda g, i: (0, 0))],
        out_specs=pl.BlockSpec((Q_TILE, group_w), lambda g, i: (i, g)),
        out_shape=jax.ShapeDtypeStruct((seq, Q_WIDTH), BF16),
        scratch_shapes=[pltpu.VMEM((n_bias, Q_ROWS, 2 * HEAD_DIM), BF16),
                        pltpu.VMEM((Q_ROWS, 2 * HEAD_DIM), F32),
                        pltpu.VMEM((Q_ROWS, V7X_LANES), F32),
                        pltpu.VMEM((Q_ROWS, SEL_TILE), F32),
                        pltpu.VMEM((Q_ROWS, HEAD_DIM), F32)],
        compiler_params=_params("parallel", "arbitrary"),
        name="nsa_attend",
    )(q, gates, o_cmp, bias, kv, kv, kv, kv, onehot)


def nsa_mixer(x, g_norm, w_in, w_o, pe, w1, w2, pre=None, next_gain=None):
    seq = x.shape[0]
    h, ssq = (rmsnorm(x, g_norm, BF16), None) if pre is None else pre
    q = matmul(h, w_in, (0,), Q_WIDTH, _ep_scale_q, BF16, norm_in=ssq, name="nsa_q")
    c0 = Q_WIDTH
    kv_c = matmul(h, w_in, (c0,), 2 * KV_WIDTH, _ep_identity, F32, norm_in=ssq, name="nsa_kv_cmp")
    c0 += 2 * KV_WIDTH
    kv = matmul(h, w_in, (c0,), 4 * KV_WIDTH, _ep_identity, BF16, norm_in=ssq, name="nsa_kv")
    c0 += 4 * KV_WIDTH
    n_gate = N_BRANCH * N_HEADS
    gates = matmul(h, w_in, (c0,), V7X_LANES, _ep_sigmoid, F32, norm_in=ssq, name="nsa_gates")
    gates = gates[:, :n_gate].reshape(seq, N_KV_HEADS, HEADS_PER_KV, N_BRANCH).transpose(1, 3, 0, 2)

    n_half = seq // CMP_STRIDE
    kv_heads = kv_c.reshape(seq, 2 * N_KV_HEADS, HEAD_DIM).transpose(1, 0, 2)
    kv_heads = kv_heads.reshape(2 * N_KV_HEADS, n_half, CMP_STRIDE * HEAD_DIM)
    kv_cmp = compress(kv_heads, pe, w1, w2)

    n_sb = seq // SEL_BLOCK
    c_start = jnp.arange(n_half) * CMP_STRIDE
    s_start = jnp.arange(n_sb) * SEL_BLOCK
    overlap = ((c_start[:, None] < s_start[None, :] + SEL_BLOCK)
               & (c_start[:, None] + CMP_BLOCK > s_start[None, :])
               & (jnp.arange(n_half)[:, None] < n_half - 1)).astype(BF16)
    e_rows = min(seq, BIAS_KEYS)
    onehot = ((jnp.arange(e_rows)[:, None] // SEL_BLOCK) == jnp.arange(BIAS_BLOCKS)[None, :]).astype(BF16)

    o = nsa_attention(q, gates, kv_cmp, kv, overlap, onehot)
    out = matmul(o, w_o, (0,), D_MODEL, _ep_residual, F32, res=x, norm_out_gain=next_gain, name="nsa_out")
    return _with_norm(out, next_gain)


def kernel(x, norm_g, final_norm_g, ffn_w_in, ffn_w_out, s5_lambda_re, s5_lambda_im, s5_log_step, s5_b_re, s5_b_im, s5_c_re, s5_c_im, s5_d, s5_w_glu, nsa_w_in, nsa_w_o, nsa_pe_k, nsa_w1_k, nsa_w2_k, nsa_pe_v, nsa_w1_v, nsa_w2_v):
    bsz, seq, d = x.shape
    assert bsz == 1 and d == D_MODEL and seq % (SCAN_SEGS * 128) == 0
    xs = x.reshape(seq, d).astype(F32)
    ffn_w_out = ffn_w_out.astype(BF16)
    s5_w_glu = s5_w_glu.astype(BF16)
    nsa_w_in = nsa_w_in.astype(BF16)
    nsa_w_o = nsa_w_o.astype(BF16)
    half = (CMP_BLOCK // 2) * HEAD_DIM
    pe = jnp.stack([nsa_pe_k, nsa_pe_v], axis=1).astype(F32).reshape(-1, 2, 2, half)
    w1 = jnp.stack([nsa_w1_k, nsa_w1_v], axis=1).astype(BF16)
    w2 = jnp.stack([nsa_w2_k, nsa_w2_v], axis=1).astype(BF16)

    pre = None
    for layer in range(DEPTH):
        j = layer // 2
        is_s5 = layer % 2 == 0
        g_ffn1, g_mix, g_ffn2 = norm_g[layer, 0], norm_g[layer, 1], norm_g[layer, 2]
        xs, pre = ffn(xs, g_ffn1, ffn_w_in, ffn_w_out, (layer, 0), pre, next_gain=None if is_s5 else g_mix)
        if is_s5:
            xs, pre = s5_mixer(xs, g_mix, s5_lambda_re[j], s5_lambda_im[j], s5_log_step[j], s5_b_re[j], s5_b_im[j],
                               s5_c_re[j], s5_c_im[j], s5_d[j], s5_w_glu[j], next_gain=g_ffn2)
        else:
            xs, pre = nsa_mixer(xs, g_mix, nsa_w_in[j], nsa_w_o[j], pe[j], w1[j], w2[j], pre, next_gain=g_ffn2)
        g_next = norm_g[layer + 1, 0] if layer + 1 < DEPTH else None
        xs, pre = ffn(xs, g_ffn2, ffn_w_in, ffn_w_out, (layer, 1), pre, next_gain=g_next)
    return rmsnorm(xs, final_norm_g, x.dtype).reshape(bsz, seq, d)
```

```python
import functools
import math

import jax
import jax.numpy as jnp
from jax import lax
from jax.experimental import pallas as pl
from jax.experimental.pallas import tpu as pltpu

F32 = jnp.float32
BF16 = jnp.bfloat16

D_MODEL = 4096
DEPTH = 4
FFN_DIM = 3 * D_MODEL // 2
NORM_EPS = 1e-6
SSM_GROUP = 16
SSM_GROUPS = D_MODEL // SSM_GROUP
SSM_STATE = 64
HEAD_DIM = 128
N_HEADS = D_MODEL // HEAD_DIM
N_KV_HEADS = 4
HEADS_PER_KV = N_HEADS // N_KV_HEADS
CMP_BLOCK = 32
CMP_STRIDE = 16
CMP_HIDDEN = 256
SEL_BLOCK = 64
N_SEL = 16
WINDOW = 512
N_BRANCH = 3
Q_WIDTH = N_HEADS * HEAD_DIM
KV_WIDTH = N_KV_HEADS * HEAD_DIM
NEG_INF = -1e30
FORCE_SCORE = 1e4

V7X_SUBLANES = 8
V7X_LANES = 128
V7X_VMEM_BYTES = 64 * 1024 * 1024
VMEM_LIMIT = V7X_VMEM_BYTES - 6 * 1024 * 1024

SCAN_SEGS = V7X_SUBLANES
SCAN_SUPER = 8
SCAN_COLS = SCAN_SUPER * SSM_STATE
SCAN_CH = SCAN_SUPER * SSM_GROUP
N_SUPER = SSM_GROUPS // SCAN_SUPER
SCAN_CHUNK = 16

Q_TILE = 128
Q_ROWS = Q_TILE * HEADS_PER_KV
SEL_TILE = 1024
CMP_BUCKET = 256
WIN_ROWS = 1024
CMP_ROWS = 1024
SEL_ROWS = 256
BIAS_BLOCKS = 128
BIAS_KEYS = BIAS_BLOCKS * SEL_BLOCK


def _params(*sem):
    return pltpu.CompilerParams(dimension_semantics=sem, vmem_limit_bytes=VMEM_LIMIT)


def _gelu(x):
    c = math.sqrt(2.0 / math.pi)
    return 0.5 * x * (1.0 + jnp.tanh(c * (x + 0.044715 * (x * x * x))))


def _dot(a, b):
    return jnp.dot(a, b, preferred_element_type=F32)


def _dot_nt(a, b):
    return lax.dot_general(a, b, (((1,), (1,)), ((), ())), preferred_element_type=F32)


def _rmsnorm_kernel(x_ref, g_ref, o_ref):
    x = x_ref[...]
    ms = jnp.mean(x * x, axis=-1, keepdims=True)
    o_ref[...] = ((x * lax.rsqrt(ms + NORM_EPS)) * g_ref[...]).astype(o_ref.dtype)


def rmsnorm(x, g, out_dtype, tm=256):
    m, d = x.shape
    return pl.pallas_call(
        _rmsnorm_kernel,
        grid=(m // tm,),
        in_specs=[pl.BlockSpec((tm, d), lambda i: (i, 0)),
                  pl.BlockSpec((1, d), lambda i: (0, 0))],
        out_specs=pl.BlockSpec((tm, d), lambda i: (i, 0)),
        out_shape=jax.ShapeDtypeStruct((m, d), out_dtype),
        compiler_params=_params("parallel"),
        name="rmsnorm",
    )(x, g.reshape(1, d).astype(F32))


def _mm_kernel(*refs, n_rhs, has_res, has_norm_in, has_norm_out, epilogue):
    refs = iter(refs)
    lhs = next(refs)[...]
    rhs_refs = [next(refs) for _ in range(n_rhs)]
    res = next(refs)[...] if has_res else None
    ssq_in_ref = next(refs) if has_norm_in else None
    gain_ref = next(refs) if has_norm_out else None
    o_ref = next(refs)
    accs = [_dot(lhs, r[...].astype(lhs.dtype)) for r in rhs_refs]
    if has_norm_in:
        inv_rms = lax.rsqrt(ssq_in_ref[:, :1] * (1.0 / lhs.shape[1]) + NORM_EPS)
        accs = [a * inv_rms for a in accs]
    out = epilogue(accs, res)
    o_ref[...] = out.astype(o_ref.dtype)
    if has_norm_out:
        xg_ref, ssq_ref = next(refs), next(refs)
        xg_ref[...] = (out * gain_ref[...]).astype(xg_ref.dtype)
        part = jnp.broadcast_to(jnp.sum(out * out, axis=1, keepdims=True), ssq_ref.shape)
        first = pl.program_id(1) == 0

        @pl.when(first)
        def _():
            ssq_ref[...] = part

        @pl.when(jnp.logical_not(first))
        def _():
            ssq_ref[...] += part


def matmul(lhs, rhs, rhs_col_offsets, n_out, epilogue, out_dtype, res=None, rhs_lead=(), norm_in=None,
           norm_out_gain=None, tm=1024, tn=512, name="matmul"):
    m, k = lhs.shape
    tn = min(tn, n_out)
    tm = min(tm, m)
    n_rhs = len(rhs_col_offsets)
    lhs_mode = pl.Buffered(1) if rhs.dtype == F32 else None
    in_specs = [pl.BlockSpec((tm, k), lambda i, j: (i, 0), pipeline_mode=lhs_mode)]
    lead_block = (None,) * len(rhs_lead)
    for off in rhs_col_offsets:
        in_specs.append(pl.BlockSpec(lead_block + (k, tn),
                                     functools.partial(lambda i, j, ob: rhs_lead + (0, j + ob), ob=off // tn)))
    args = [lhs] + [rhs] * n_rhs
    tile_spec = pl.BlockSpec((tm, tn), lambda i, j: (i, j))
    row_spec = pl.BlockSpec((tm, V7X_LANES), lambda i, j: (i, 0))
    if res is not None:
        in_specs.append(tile_spec)
        args.append(res)
    if norm_in is not None:
        in_specs.append(row_spec)
        args.append(norm_in)
    out_specs, out_shape = tile_spec, jax.ShapeDtypeStruct((m, n_out), out_dtype)
    if norm_out_gain is not None:
        in_specs.append(pl.BlockSpec((1, tn), lambda i, j: (0, j)))
        args.append(norm_out_gain.reshape(1, n_out).astype(F32))
        out_specs = [tile_spec, tile_spec, row_spec]
        out_shape = [out_shape, jax.ShapeDtypeStruct((m, n_out), BF16), jax.ShapeDtypeStruct((m, V7X_LANES), F32)]
    return pl.pallas_call(
        functools.partial(_mm_kernel, n_rhs=n_rhs, has_res=res is not None, has_norm_in=norm_in is not None,
                          has_norm_out=norm_out_gain is not None, epilogue=epilogue),
        grid=(m // tm, n_out // tn),
        in_specs=in_specs,
        out_specs=out_specs,
        out_shape=out_shape,
        compiler_params=_params("parallel", "arbitrary"),
        name=name,
    )(*args)


def _ep_swiglu(accs, res):
    a, b = accs
    return (a * jax.nn.sigmoid(a)) * b


def _ep_half_residual(accs, res):
    return res + 0.5 * accs[0]


def _ep_residual(accs, res):
    return res + accs[0]


def _ep_glu_residual(accs, res):
    val, gate = accs
    return res + val * jax.nn.sigmoid(gate)


def _ep_scale_q(accs, res):
    return accs[0] * (HEAD_DIM ** -0.5 * math.log2(math.e))


def _ep_identity(accs, res):
    return accs[0]


def _ep_sigmoid(accs, res):
    return jax.nn.sigmoid(accs[0])


def _with_norm(result, next_gain):
    return (result[0], (result[1], result[2])) if next_gain is not None else (result, None)


def ffn(x, g, w_in, w_out, lead=(), pre=None, next_gain=None):
    if pre is None:
        act = matmul(rmsnorm(x, g, BF16), w_in, (0, FFN_DIM), FFN_DIM, _ep_swiglu, BF16, rhs_lead=lead, name="ffn_in")
    else:
        act = matmul(pre[0], w_in, (0, FFN_DIM), FFN_DIM, _ep_swiglu, BF16, rhs_lead=lead, norm_in=pre[1],
                     name="ffn_in")
    out = matmul(act, w_out, (0,), D_MODEL, _ep_half_residual, F32, res=x, rhs_lead=lead, norm_out_gain=next_gain,
                 name="ffn_out")
    return _with_norm(out, next_gain)


def _s5_discretise(lam_re, lam_im, log_step, b_re, b_im, seg_len):
    dt = jnp.exp(log_step.astype(F32))[:, None]
    lr, li = lam_re.astype(F32), lam_im.astype(F32)
    mag = jnp.exp(lr * dt)
    ang = li * dt
    ab_re, ab_im = mag * jnp.cos(ang), mag * jnp.sin(ang)
    den = lr * lr + li * li
    nr, ni = ab_re - 1.0, ab_im
    coef_re = (nr * lr + ni * li) / den
    coef_im = (ni * lr - nr * li) / den
    br, bi = b_re.astype(F32), b_im.astype(F32)
    bb_re = coef_re[..., None] * br - coef_im[..., None] * bi
    bb_im = coef_re[..., None] * bi + coef_im[..., None] * br
    mag_n = jnp.exp(seg_len * (lr * dt))
    ang_n = seg_len * ang
    return ab_re, ab_im, bb_re, bb_im, mag_n * jnp.cos(ang_n), mag_n * jnp.sin(ang_n)


def _s5_layout(ab_re, ab_im, bb_re, bb_im, c_re, c_im, an_re, an_im):
    eye = jnp.eye(SCAN_SUPER, dtype=F32)

    def in_proj(bb):
        w = bb.reshape(N_SUPER, SCAN_SUPER, SSM_STATE, SSM_GROUP).transpose(0, 1, 3, 2)
        w = jnp.einsum('sghp,gk->sghkp', w, eye)
        return w.reshape(N_SUPER, SCAN_CH, SCAN_COLS)

    def out_proj(c):
        w = c.reshape(N_SUPER, SCAN_SUPER, SSM_GROUP, SSM_STATE).transpose(0, 1, 3, 2)
        w = jnp.einsum('sgph,gk->sgpkh', w, eye)
        return w.reshape(N_SUPER, SCAN_COLS, SCAN_CH)

    w_b = jnp.concatenate([in_proj(bb_re), in_proj(bb_im)], axis=2).astype(BF16)
    w_c = jnp.concatenate([out_proj(c_re.astype(F32)), -out_proj(c_im.astype(F32))], axis=1).astype(BF16)

    def lanes(a):
        a = a.reshape(N_SUPER, 1, SCAN_COLS)
        return jnp.broadcast_to(a, (N_SUPER, SCAN_SEGS, SCAN_COLS))

    return w_b, w_c, lanes(ab_re), lanes(ab_im), an_re.reshape(N_SUPER, SCAN_COLS), an_im.reshape(N_SUPER, SCAN_COLS)


def _s5_chunk_pipeline(u_ref, wb_ref, wc_ref, d_ref, g_ref, bu_scr, h_scr, a_re, a_im, h_re, h_im, tb):
    n_chunks = tb // SCAN_CHUNK
    chunk_rows = SCAN_CHUNK * SCAN_SEGS

    def in_proj(k):
        rows = slice(k * chunk_rows, (k + 1) * chunk_rows)
        bu_scr[rows, :] = _dot(u_ref[rows, :].astype(BF16), wb_ref[0])

    def out_proj(k):
        rows = slice(k * chunk_rows, (k + 1) * chunk_rows)
        y = _dot(h_scr[rows, :].astype(BF16), wc_ref[0]) + d_ref[...] * u_ref[rows, :]
        g_ref[rows, :] = _gelu(y).astype(g_ref.dtype)

    in_proj(0)
    for k in range(n_chunks):
        if k + 1 < n_chunks:
            in_proj(k + 1)
        for t in range(k * SCAN_CHUNK, (k + 1) * SCAN_CHUNK):
            rows = slice(t * SCAN_SEGS, (t + 1) * SCAN_SEGS)
            bu = bu_scr[rows, :]
            h_re, h_im = ((a_re * h_re - a_im * h_im) + bu[:, :SCAN_COLS],
                          (a_re * h_im + a_im * h_re) + bu[:, SCAN_COLS:])
            if h_scr is not None:
                h_scr[rows, :SCAN_COLS] = h_re
                h_scr[rows, SCAN_COLS:] = h_im
        if h_scr is not None and k > 0:
            out_proj(k - 1)
    if h_scr is not None:
        out_proj(n_chunks - 1)
    return h_re, h_im


def _s5_state_kernel(u_ref, wb_ref, are_ref, aim_ref, hend_ref, bu_scr, st_scr, *, tb):
    tblk = pl.program_id(1)

    @pl.when(tblk == 0)
    def _():
        st_scr[...] = jnp.zeros_like(st_scr)

    st = st_scr[...]
    hr, hi = _s5_chunk_pipeline(u_ref, wb_ref, None, None, None, bu_scr, None, are_ref[0], aim_ref[0],
                                st[:, :SCAN_COLS], st[:, SCAN_COLS:], tb)
    st_scr[:, :SCAN_COLS] = hr
    st_scr[:, SCAN_COLS:] = hi

    @pl.when(tblk == pl.num_programs(1) - 1)
    def _():
        hend_ref[0] = st_scr[...]


def _s5_carry_kernel(hend_ref, an_re_ref, an_im_ref, s_ref):
    ar, ai = an_re_ref[...], an_im_ref[...]
    sr = jnp.zeros_like(ar)
    si = jnp.zeros_like(ar)
    for c in range(SCAN_SEGS):
        s_ref[c, :, :SCAN_COLS] = sr
        s_ref[c, :, SCAN_COLS:] = si
        hr = hend_ref[c, :, :SCAN_COLS]
        hi = hend_ref[c, :, SCAN_COLS:]
        sr, si = (ar * sr - ai * si) + hr, (ar * si + ai * sr) + hi


def _s5_output_kernel(u_ref, wb_ref, wc_ref, are_ref, aim_ref, s0_ref, d_ref, g_ref, bu_scr, h_scr, st_scr, *, tb):
    tblk = pl.program_id(1)

    @pl.when(tblk == 0)
    def _():
        st_scr[...] = s0_ref[0]

    st = st_scr[...]
    hr, hi = _s5_chunk_pipeline(u_ref, wb_ref, wc_ref, d_ref, g_ref, bu_scr, h_scr, are_ref[0], aim_ref[0],
                                st[:, :SCAN_COLS], st[:, SCAN_COLS:], tb)
    st_scr[:, :SCAN_COLS] = hr
    st_scr[:, SCAN_COLS:] = hi


def s5_mixer(x, g_norm, lam_re, lam_im, log_step, b_re, b_im, c_re, c_im, d_skip, w_glu, next_gain=None, tb=128):
    seq = x.shape[0]
    seg_len = seq // SCAN_SEGS
    tb = min(tb, seg_len)
    assert tb % SCAN_CHUNK == 0
    rows = tb * SCAN_SEGS
    n_tblk = seg_len // tb
    u = _to_segment_rows(rmsnorm(x, g_norm, F32))
    disc = _s5_discretise(lam_re, lam_im, log_step, b_re, b_im, seg_len)
    w_b, w_c, a_re, a_im, an_re, an_im = _s5_layout(disc[0], disc[1], disc[2], disc[3], c_re, c_im, disc[4], disc[5])

    u_spec = pl.BlockSpec((rows, SCAN_CH), lambda s, t: (t, s))
    wb_spec = pl.BlockSpec((1, SCAN_CH, 2 * SCAN_COLS), lambda s, t: (s, 0, 0))
    a_spec = pl.BlockSpec((1, SCAN_SEGS, SCAN_COLS), lambda s, t: (s, 0, 0))
    st_spec = pl.BlockSpec((1, SCAN_SEGS, 2 * SCAN_COLS), lambda s, t: (s, 0, 0))

    h_end = pl.pallas_call(
        functools.partial(_s5_state_kernel, tb=tb),
        grid=(N_SUPER, n_tblk),
        in_specs=[u_spec, wb_spec, a_spec, a_spec],
        out_specs=st_spec,
        out_shape=jax.ShapeDtypeStruct((N_SUPER, SCAN_SEGS, 2 * SCAN_COLS), F32),
        scratch_shapes=[pltpu.VMEM((rows, 2 * SCAN_COLS), F32), pltpu.VMEM((SCAN_SEGS, 2 * SCAN_COLS), F32)],
        compiler_params=_params("parallel", "arbitrary"),
        name="s5_state",
    )(u, w_b, a_re, a_im)

    s0 = pl.pallas_call(
        _s5_carry_kernel,
        out_shape=jax.ShapeDtypeStruct((SCAN_SEGS, N_SUPER, 2 * SCAN_COLS), F32),
        name="s5_carry",
    )(h_end.transpose(1, 0, 2), an_re, an_im)
    s0 = s0.transpose(1, 0, 2)

    act = pl.pallas_call(
        functools.partial(_s5_output_kernel, tb=tb),
        grid=(N_SUPER, n_tblk),
        in_specs=[u_spec, wb_spec,
                  pl.BlockSpec((1, 2 * SCAN_COLS, SCAN_CH), lambda s, t: (s, 0, 0)),
                  a_spec, a_spec, st_spec,
                  pl.BlockSpec((1, SCAN_CH), lambda s, t: (0, s))],
        out_specs=pl.BlockSpec((rows, SCAN_CH), lambda s, t: (t, s)),
        out_shape=jax.ShapeDtypeStruct((seq, D_MODEL), BF16),
        scratch_shapes=[pltpu.VMEM((rows, 2 * SCAN_COLS), F32), pltpu.VMEM((rows, 2 * SCAN_COLS), F32),
                        pltpu.VMEM((SCAN_SEGS, 2 * SCAN_COLS), F32)],
        compiler_params=_params("parallel", "arbitrary"),
        name="s5_output",
    )(u, w_b, w_c, a_re, a_im, s0, d_skip.reshape(1, D_MODEL).astype(F32))

    out = matmul(_from_segment_rows(act), w_glu, (0, D_MODEL), D_MODEL, _ep_glu_residual, F32, res=x,
                 norm_out_gain=next_gain, name="s5_glu")
    return _with_norm(out, next_gain)


def _to_segment_rows(x):
    seq, d = x.shape
    return x.reshape(SCAN_SEGS, seq // SCAN_SEGS, d).transpose(1, 0, 2).reshape(seq, d)


def _from_segment_rows(xp):
    seq, d = xp.shape
    return xp.reshape(seq // SCAN_SEGS, SCAN_SEGS, d).transpose(1, 0, 2).reshape(seq, d)


def _compress_kernel(x_ref, pe_ref, w1_ref, w2_ref, o_ref):
    x = x_ref[0]
    n_half = x.shape[0]
    half = (CMP_BLOCK // 2) * HEAD_DIM
    pe = pe_ref[0]
    xa = (x + pe[0:1, :]).astype(BF16)
    xb = (x + pe[1:2, :]).astype(BF16)
    top = _dot(xa, w1_ref[0, :half, :])
    bot = _dot(xb, w1_ref[0, half:, :])
    bot_next = pltpu.roll(bot, n_half - 1, 0)
    hid = _gelu(top + bot_next)
    out = _dot(hid.astype(BF16), w2_ref[0])
    row = lax.broadcasted_iota(jnp.int32, out.shape, 0)
    o_ref[0] = jnp.where(row < n_half - 1, out, 0.0).astype(o_ref.dtype)


def compress(kv_heads, pe, w1, w2):
    n_hd, n_half, width = kv_heads.shape
    return pl.pallas_call(
        _compress_kernel,
        grid=(n_hd,),
        in_specs=[pl.BlockSpec((1, n_half, width), lambda i: (i, 0, 0)),
                  pl.BlockSpec((1, 2, width), lambda i: (i // N_KV_HEADS, 0, 0)),
                  pl.BlockSpec((1, 2 * width, CMP_HIDDEN), lambda i: (i // N_KV_HEADS, 0, 0)),
                  pl.BlockSpec((1, CMP_HIDDEN, HEAD_DIM), lambda i: (i // N_KV_HEADS, 0, 0))],
        out_specs=pl.BlockSpec((1, n_half, HEAD_DIM), lambda i: (i, 0, 0)),
        out_shape=jax.ShapeDtypeStruct((n_hd, n_half, HEAD_DIM), BF16),
        compiler_params=_params("parallel"),
        name="nsa_compress",
    )(kv_heads, pe, w1, w2)


def _split3_dot(p, w):
    p1 = p.astype(BF16)
    r1 = p - p1.astype(F32)
    p2 = r1.astype(BF16)
    p3 = (r1 - p2.astype(F32)).astype(BF16)
    return _dot(p1, w) + _dot(p2, w) + _dot(p3, w)


def _head_rows(q_blk, g):
    base = g * HEADS_PER_KV * HEAD_DIM
    return jnp.concatenate(
        [q_blk[:, base + n * HEAD_DIM:base + (n + 1) * HEAD_DIM] for n in range(HEADS_PER_KV)], axis=0)


def _select_kernel(q_ref, kc_ref, ov_ref, oc_ref, bias_ref, imp_scr, *, n_sb, n_sbp, cmp_bucket):
    qb = pl.program_id(0)
    n_cmp = kc_ref.shape[1]
    t_row = qb * Q_TILE + (lax.broadcasted_iota(jnp.int32, (CMP_ROWS, 1), 0) & (Q_TILE - 1))
    has_key = (t_row >= CMP_BLOCK - 1).astype(F32)

    def compressed_branch(n_cols):
        q_blk = q_ref[...]
        c_end = lax.broadcasted_iota(jnp.int32, (1, n_cols), 1) * CMP_STRIDE + (CMP_BLOCK - 1)
        visible = c_end <= t_row
        heads = CMP_ROWS // Q_TILE
        for g in range(N_KV_HEADS):
            p_sum = None
            for n0 in range(0, HEADS_PER_KV, heads):
                col0 = (g * HEADS_PER_KV + n0) * HEAD_DIM
                qr = jnp.concatenate([q_blk[:, col0 + n * HEAD_DIM:col0 + (n + 1) * HEAD_DIM] for n in range(heads)],
                                     axis=0)
                s = jnp.where(visible, _dot_nt(qr, kc_ref[g, :n_cols, :]), NEG_INF)
                e = jnp.exp2(s - jnp.max(s, axis=1, keepdims=True))
                p_c = e * (has_key / jnp.sum(e, axis=1, keepdims=True))
                o_c = _dot(p_c.astype(BF16), kc_ref[N_KV_HEADS + g, :n_cols, :])
                for n in range(heads):
                    oc_ref[:, col0 + n * HEAD_DIM:col0 + (n + 1) * HEAD_DIM] = o_c[n * Q_TILE:(n + 1) * Q_TILE]
                    part = p_c[n * Q_TILE:(n + 1) * Q_TILE]
                    p_sum = part if p_sum is None else p_sum + part
            imp_scr[g * Q_TILE:(g + 1) * Q_TILE, :] = _split3_dot(p_sum, ov_ref[:n_cols, :])

    last_visible = (Q_TILE // CMP_STRIDE) * qb + (Q_TILE - CMP_BLOCK) // CMP_STRIDE
    for k in range(n_cmp // cmp_bucket):
        pl.when(last_visible // cmp_bucket == k)(functools.partial(compressed_branch, (k + 1) * cmp_bucket))

    rows = N_KV_HEADS * Q_TILE
    t_q = qb * Q_TILE + (lax.broadcasted_iota(jnp.int32, (rows, 1), 0) & (Q_TILE - 1))
    blk = t_q >> int(math.log2(SEL_BLOCK))

    def pick_blocks(n_j):
        imp = imp_scr[:, :n_j]
        j_i = lax.broadcasted_iota(jnp.int32, (1, n_j), 1)
        forced = jnp.where(j_i == 0, 1.0, jnp.where(j_i == blk, 1.0, jnp.where(j_i == blk - 1, 1.0, 0.0)))
        work = jnp.where(forced > 0.0, -3e38, jnp.where(j_i <= blk, imp, -1.0))
        picked = forced
        for _ in range(min(N_SEL, n_sb) - 3):
            hit = j_i == jnp.argmax(work, axis=1, keepdims=True).astype(jnp.int32)
            picked = jnp.where(hit, 1.0, picked)
            work = jnp.where(hit, -3e38, work)
        bias = jnp.where(picked > 0.0, jnp.where(j_i < 2 * qb, 0.0, NEG_INF), NEG_INF).astype(BF16)
        if n_sbp > n_j:
            bias = jnp.concatenate([bias, jnp.full((rows, n_sbp - n_j), NEG_INF, BF16)], axis=1)
        for g in range(N_KV_HEADS):
            bias_ref[:, g * n_sbp:(g + 1) * n_sbp] = bias[g * Q_TILE:(g + 1) * Q_TILE]

    n_causal = (Q_TILE // SEL_BLOCK) * (qb + 1)
    if n_sb > BIAS_BLOCKS:
        pl.when(n_causal <= BIAS_BLOCKS)(functools.partial(pick_blocks, BIAS_BLOCKS))
        pl.when(n_causal > BIAS_BLOCKS)(functools.partial(pick_blocks, n_sb))
    else:
        pick_blocks(n_sb)


def _attend_kernel(q_ref, gate_ref, oc_ref, bias_ref, ks_ref, vs_ref, kw_ref, vw_ref, e_ref, o_ref,
                   qa_scr, acc_scr, m_scr, s_scr, ow_scr, *, n_bias, tiles_per_bias):
    qb = pl.program_id(1)
    qr = _head_rows(q_ref[...], 0)
    t_rel = lax.broadcasted_iota(jnp.int32, (Q_ROWS, 1), 0) & (Q_TILE - 1)
    t_row = qb * Q_TILE + t_rel
    ones_v = jnp.ones((max(WINDOW + Q_TILE, SEL_TILE), HEAD_DIM), BF16)

    bias = bias_ref[...]
    for a in range(n_bias):
        qa_scr[a, :, :HEAD_DIM] = qr
        for n in range(HEADS_PER_KV):
            qa_scr[a, n * Q_TILE:(n + 1) * Q_TILE, HEAD_DIM:] = bias[:, a * BIAS_BLOCKS:(a + 1) * BIAS_BLOCKS]

    k0 = pl.multiple_of(qb * Q_TILE, Q_TILE)
    s = _dot_nt(qr, ks_ref[pl.ds(k0, Q_TILE), :])
    k_rel = lax.broadcasted_iota(jnp.int32, (1, Q_TILE), 1)
    s = jnp.where(k_rel <= t_rel, s, NEG_INF)
    m0 = jnp.max(s, axis=1, keepdims=True)
    p = jnp.exp2(s - m0).astype(BF16)
    acc_scr[...] = _dot(p, jnp.concatenate([vs_ref[pl.ds(k0, Q_TILE), :], ones_v[:Q_TILE]], axis=1))
    m_scr[...] = jnp.broadcast_to(m0, (Q_ROWS, V7X_LANES))

    band = WINDOW + Q_TILE
    b0 = pl.multiple_of(jnp.maximum(qb * Q_TILE - WINDOW, 0), Q_TILE)
    k_band = kw_ref[pl.ds(b0, band), :]
    v_band = jnp.concatenate([vw_ref[pl.ds(b0, band), :], ones_v[:band]], axis=1)
    k_pos = b0 + lax.broadcasted_iota(jnp.int32, (1, band), 1)
    for c in range(Q_ROWS // WIN_ROWS):
        rows = slice(c * WIN_ROWS, (c + 1) * WIN_ROWS)
        t_c = t_row[rows]
        s = _dot_nt(qr[rows], k_band)
        s = jnp.where(k_pos <= t_c, jnp.where(k_pos > t_c - WINDOW, s, NEG_INF), NEG_INF)
        p = jnp.exp2(s - jnp.max(s, axis=1, keepdims=True)).astype(BF16)
        acc_w = _dot(p, v_band)
        ow_scr[rows, :] = acc_w[:, :HEAD_DIM] / acc_w[:, HEAD_DIM:]

    n_tiles = (qb * Q_TILE + SEL_TILE - 1) // SEL_TILE
    chunks = [pl.ds(c * SEL_ROWS, SEL_ROWS) for c in range(Q_ROWS // SEL_ROWS)]

    def keys_of(i):
        r0 = pl.multiple_of(i * SEL_TILE, SEL_TILE)
        a = i // tiles_per_bias
        e0 = pl.multiple_of((i - a * tiles_per_bias) * SEL_TILE, SEL_TILE)
        return a, jnp.concatenate([ks_ref[pl.ds(r0, SEL_TILE), :], e_ref[pl.ds(e0, SEL_TILE), :]], axis=1)

    a_first, k_first = keys_of(0)
    for rows in chunks:
        s_scr[rows, :] = _dot_nt(qa_scr[a_first, rows, :], k_first)

    def sel_step(i, has_next):
        r0 = pl.multiple_of(i * SEL_TILE, SEL_TILE)
        v_aug = jnp.concatenate([vs_ref[pl.ds(r0, SEL_TILE), :], ones_v[:SEL_TILE]], axis=1)
        if has_next:
            a_next, k_next = keys_of(i + 1)
        for rows in chunks:
            st = s_scr[rows, :]
            m_old = m_scr[rows, :]
            m_new = jnp.maximum(m_old, jnp.max(st, axis=1, keepdims=True))
            pt = jnp.exp2(st - jnp.concatenate([m_new] * (SEL_TILE // V7X_LANES), axis=1)).astype(BF16)
            alpha = jnp.exp2(m_old - m_new)
            acc_scr[rows, :] = jnp.concatenate([alpha, alpha], axis=1) * acc_scr[rows, :] + _dot(pt, v_aug)
            m_scr[rows, :] = m_new
            if has_next:
                s_scr[rows, :] = _dot_nt(qa_scr[a_next, rows, :], k_next)

    n_piped = jnp.maximum(n_tiles - 1, 0)

    def sel_pair(ii, carry):
        sel_step(2 * ii, True)
        sel_step(2 * ii + 1, True)
        return carry

    lax.fori_loop(0, n_piped // 2, sel_pair, 0)

    @pl.when(n_piped % 2 == 1)
    def _():
        sel_step(n_piped - 1, True)

    @pl.when(n_tiles > 0)
    def _():
        sel_step(n_tiles - 1, False)

    gates = gate_ref[0]
    for n in range(HEADS_PER_KV):
        rows = pl.ds(n * Q_TILE, Q_TILE)
        cols = slice(n * HEAD_DIM, (n + 1) * HEAD_DIM)
        o_s = acc_scr[rows, :HEAD_DIM] / acc_scr[rows, HEAD_DIM:]
        o_n = (gates[0, :, n:n + 1] * oc_ref[:, cols] + gates[1, :, n:n + 1] * o_s
               + gates[2, :, n:n + 1] * ow_scr[rows, :])
        o_ref[:, cols] = o_n.astype(o_ref.dtype)


def nsa_attention(q, gates, kv_cmp, kv, overlap, onehot):
    seq = q.shape[0]
    n_cmp = kv_cmp.shape[1]
    n_sb = seq // SEL_BLOCK
    n_bias = pl.cdiv(n_sb, BIAS_BLOCKS)
    n_sbp = n_bias * BIAS_BLOCKS
    e_rows = onehot.shape[0]
    group_w = HEADS_PER_KV * HEAD_DIM

    o_cmp, bias = pl.pallas_call(
        functools.partial(_select_kernel, n_sb=n_sb, n_sbp=n_sbp, cmp_bucket=min(CMP_BUCKET, n_cmp)),
        grid=(seq // Q_TILE,),
        in_specs=[pl.BlockSpec((Q_TILE, Q_WIDTH), lambda i: (i, 0)),
                  pl.BlockSpec((2 * N_KV_HEADS, n_cmp, HEAD_DIM), lambda i: (0, 0, 0)),
                  pl.BlockSpec((n_cmp, n_sb), lambda i: (0, 0))],
        out_specs=[pl.BlockSpec((Q_TILE, Q_WIDTH), lambda i: (i, 0)),
                   pl.BlockSpec((Q_TILE, N_KV_HEADS * n_sbp), lambda i: (i, 0))],
        out_shape=[jax.ShapeDtypeStruct((seq, Q_WIDTH), F32),
                   jax.ShapeDtypeStruct((seq, N_KV_HEADS * n_sbp), BF16)],
        scratch_shapes=[pltpu.VMEM((N_KV_HEADS * Q_TILE, n_sb), F32)],
        compiler_params=_params("parallel"),
        name="nsa_select",
    )(q, kv_cmp, overlap)

    resident = pl.Buffered(1)
    kv_spec = lambda col: pl.BlockSpec((seq, HEAD_DIM), functools.partial(lambda g, i, c: (0, c + g), c=col),
                                       pipeline_mode=resident)
    return pl.pallas_call(
        functools.partial(_attend_kernel, n_bias=n_bias, tiles_per_bias=e_rows // SEL_TILE),
        grid=(N_KV_HEADS, seq // Q_TILE),
        in_specs=[pl.BlockSpec((Q_TILE, group_w), lambda g, i: (i, g)),
                  pl.BlockSpec((1, N_BRANCH, Q_TILE, HEADS_PER_KV), lambda g, i: (g, 0, i, 0)),
                  pl.BlockSpec((Q_TILE, group_w), lambda g, i: (i, g)),
                  pl.BlockSpec((Q_TILE, n_sbp), lambda g, i: (i, g)),
                  kv_spec(0), kv_spec(N_KV_HEADS), kv_spec(2 * N_KV_HEADS), kv_spec(3 * N_KV_HEADS),
                  pl.BlockSpec((e_rows, BIAS_BLOCKS), lambda g, i: (0, 0))],
        out_specs=pl.BlockSpec((Q_TILE, group_w), lambda g, i: (i, g)),
        out_shape=jax.ShapeDtypeStruct((seq, Q_WIDTH), BF16),
        scratch_shapes=[pltpu.VMEM((n_bias, Q_ROWS, 2 * HEAD_DIM), BF16),
                        pltpu.VMEM((Q_ROWS, 2 * HEAD_DIM), F32),
                        pltpu.VMEM((Q_ROWS, V7X_LANES), F32),
                        pltpu.VMEM((Q_ROWS, SEL_TILE), F32),
                        pltpu.VMEM((Q_ROWS, HEAD_DIM), F32)],
        compiler_params=_params("parallel", "arbitrary"),
        name="nsa_attend",
    )(q, gates, o_cmp, bias, kv, kv, kv, kv, onehot)


def nsa_mixer(x, g_norm, w_in, w_o, pe, w1, w2, pre=None, next_gain=None):
    seq = x.shape[0]
    h, ssq = (rmsnorm(x, g_norm, BF16), None) if pre is None else pre
    q = matmul(h, w_in, (0,), Q_WIDTH, _ep_scale_q, BF16, norm_in=ssq, name="nsa_q")
    c0 = Q_WIDTH
    kv_c = matmul(h, w_in, (c0,), 2 * KV_WIDTH, _ep_identity, F32, norm_in=ssq, name="nsa_kv_cmp")
    c0 += 2 * KV_WIDTH
    kv = matmul(h, w_in, (c0,), 4 * KV_WIDTH, _ep_identity, BF16, norm_in=ssq, name="nsa_kv")
    c0 += 4 * KV_WIDTH
    n_gate = N_BRANCH * N_HEADS
    gates = matmul(h, w_in, (c0,), V7X_LANES, _ep_sigmoid, F32, norm_in=ssq, name="nsa_gates")
    gates = gates[:, :n_gate].reshape(seq, N_KV_HEADS, HEADS_PER_KV, N_BRANCH).transpose(1, 3, 0, 2)

    n_half = seq // CMP_STRIDE
    kv_heads = kv_c.reshape(seq, 2 * N_KV_HEADS, HEAD_DIM).transpose(1, 0, 2)
    kv_heads = kv_heads.reshape(2 * N_KV_HEADS, n_half, CMP_STRIDE * HEAD_DIM)
    kv_cmp = compress(kv_heads, pe, w1, w2)

    n_sb = seq // SEL_BLOCK
    c_start = jnp.arange(n_half) * CMP_STRIDE
    s_start = jnp.arange(n_sb) * SEL_BLOCK
    overlap = ((c_start[:, None] < s_start[None, :] + SEL_BLOCK)
               & (c_start[:, None] + CMP_BLOCK > s_start[None, :])
               & (jnp.arange(n_half)[:, None] < n_half - 1)).astype(BF16)
    e_rows = min(seq, BIAS_KEYS)
    onehot = ((jnp.arange(e_rows)[:, None] // SEL_BLOCK) == jnp.arange(BIAS_BLOCKS)[None, :]).astype(BF16)

    o = nsa_attention(q, gates, kv_cmp, kv, overlap, onehot)
    out = matmul(o, w_o, (0,), D_MODEL, _ep_residual, F32, res=x, norm_out_gain=next_gain, name="nsa_out")
    return _with_norm(out, next_gain)


def kernel(x, norm_g, final_norm_g, ffn_w_in, ffn_w_out, s5_lambda_re, s5_lambda_im, s5_log_step, s5_b_re, s5_b_im, s5_c_re, s5_c_im, s5_d, s5_w_glu, nsa_w_in, nsa_w_o, nsa_pe_k, nsa_w1_k, nsa_w2_k, nsa_pe_v, nsa_w1_v, nsa_w2_v):
    bsz, seq, d = x.shape
    assert bsz == 1 and d == D_MODEL and seq % (SCAN_SEGS * 128) == 0
    xs = x.reshape(seq, d).astype(F32)
    ffn_w_out = ffn_w_out.astype(BF16)
    s5_w_glu = s5_w_glu.astype(BF16)
    nsa_w_in = nsa_w_in.astype(BF16)
    nsa_w_o = nsa_w_o.astype(BF16)
    half = (CMP_BLOCK // 2) * HEAD_DIM
    pe = jnp.stack([nsa_pe_k, nsa_pe_v], axis=1).astype(F32).reshape(-1, 2, 2, half)
    w1 = jnp.stack([nsa_w1_k, nsa_w1_v], axis=1).astype(BF16)
    w2 = jnp.stack([nsa_w2_k, nsa_w2_v], axis=1).astype(BF16)

    pre = None
    for layer in range(DEPTH):
        j = layer // 2
        is_s5 = layer % 2 == 0
        g_ffn1, g_mix, g_ffn2 = norm_g[layer, 0], norm_g[layer, 1], norm_g[layer, 2]
        xs, pre = ffn(xs, g_ffn1, ffn_w_in, ffn_w_out, (layer, 0), pre, next_gain=None if is_s5 else g_mix)
        if is_s5:
            xs, pre = s5_mixer(xs, g_mix, s5_lambda_re[j], s5_lambda_im[j], s5_log_step[j], s5_b_re[j], s5_b_im[j],
                               s5_c_re[j], s5_c_im[j], s5_d[j], s5_w_glu[j], next_gain=g_ffn2)
        else:
            xs, pre = nsa_mixer(xs, g_mix, nsa_w_in[j], nsa_w_o[j], pe[j], w1[j], w2[j], pre, next_gain=g_ffn2)
        g_next = norm_g[layer + 1, 0] if layer + 1 < DEPTH else None
        xs, pre = ffn(xs, g_ffn2, ffn_w_in, ffn_w_out, (layer, 1), pre, next_gain=g_next)
    return rmsnorm(xs, final_norm_g, x.dtype).reshape(bsz, seq, d)
```

```python
import functools
import math

import jax
import jax.numpy as jnp
from jax import lax
from jax.experimental import pallas as pl
from jax.experimental.pallas import tpu as pltpu

F32 = jnp.float32
BF16 = jnp.bfloat16

D_MODEL = 4096
DEPTH = 4
FFN_DIM = 3 * D_MODEL // 2
NORM_EPS = 1e-6
SSM_GROUP = 16
SSM_GROUPS = D_MODEL // SSM_GROUP
SSM_STATE = 64
HEAD_DIM = 128
N_HEADS = D_MODEL // HEAD_DIM
N_KV_HEADS = 4
HEADS_PER_KV = N_HEADS // N_KV_HEADS
CMP_BLOCK = 32
CMP_STRIDE = 16
CMP_HIDDEN = 256
SEL_BLOCK = 64
N_SEL = 16
WINDOW = 512
N_BRANCH = 3
Q_WIDTH = N_HEADS * HEAD_DIM
KV_WIDTH = N_KV_HEADS * HEAD_DIM
NEG_INF = -1e30
FORCE_SCORE = 1e4

V7X_SUBLANES = 8
V7X_LANES = 128
V7X_VMEM_BYTES = 64 * 1024 * 1024
VMEM_LIMIT = V7X_VMEM_BYTES - 6 * 1024 * 1024

SCAN_SEGS = V7X_SUBLANES
SCAN_SUPER = 8
SCAN_COLS = SCAN_SUPER * SSM_STATE
SCAN_CH = SCAN_SUPER * SSM_GROUP
N_SUPER = SSM_GROUPS // SCAN_SUPER
SCAN_CHUNK = 16

Q_TILE = 128
Q_ROWS = Q_TILE * HEADS_PER_KV
SEL_TILE = 1024
CMP_BUCKET = 256
WIN_ROWS = 1024
CMP_ROWS = 1024
SEL_ROWS = 256
BIAS_BLOCKS = 128
BIAS_KEYS = BIAS_BLOCKS * SEL_BLOCK


def _params(*sem):
    return pltpu.CompilerParams(dimension_semantics=sem, vmem_limit_bytes=VMEM_LIMIT)


def _gelu(x):
    c = math.sqrt(2.0 / math.pi)
    return 0.5 * x * (1.0 + jnp.tanh(c * (x + 0.044715 * (x * x * x))))


def _dot(a, b):
    return jnp.dot(a, b, preferred_element_type=F32)


def _dot_nt(a, b):
    return lax.dot_general(a, b, (((1,), (1,)), ((), ())), preferred_element_type=F32)


def _rmsnorm_kernel(x_ref, g_ref, o_ref):
    x = x_ref[...]
    ms = jnp.mean(x * x, axis=-1, keepdims=True)
    o_ref[...] = ((x * lax.rsqrt(ms + NORM_EPS)) * g_ref[...]).astype(o_ref.dtype)


def _segment_block(i, blocks_per_seg):
    return i % blocks_per_seg, i // blocks_per_seg


def rmsnorm(x, g, out_dtype, tm=256, to_segment_rows=False):
    m, d = x.shape
    if to_segment_rows:
        per_seg = m // SCAN_SEGS // tm
        out_specs = pl.BlockSpec((tm, d), lambda i: _segment_block(i, per_seg))
        out_shape = jax.ShapeDtypeStruct((m // SCAN_SEGS, SCAN_SEGS * d), out_dtype)
    else:
        out_specs = pl.BlockSpec((tm, d), lambda i: (i, 0))
        out_shape = jax.ShapeDtypeStruct((m, d), out_dtype)
    out = pl.pallas_call(
        _rmsnorm_kernel,
        grid=(m // tm,),
        in_specs=[pl.BlockSpec((tm, d), lambda i: (i, 0)),
                  pl.BlockSpec((1, d), lambda i: (0, 0))],
        out_specs=out_specs,
        out_shape=out_shape,
        compiler_params=_params("parallel"),
        name="rmsnorm",
    )(x, g.reshape(1, d).astype(F32))
    return out.reshape(m, d)


def _mm_kernel(*refs, n_rhs, has_res, has_norm_in, has_norm_out, epilogue):
    refs = iter(refs)
    lhs = next(refs)[...]
    rhs_refs = [next(refs) for _ in range(n_rhs)]
    res = next(refs)[...] if has_res else None
    ssq_in_ref = next(refs) if has_norm_in else None
    gain_ref = next(refs) if has_norm_out else None
    o_ref = next(refs)
    accs = [_dot(lhs, r[...].astype(lhs.dtype)) for r in rhs_refs]
    if has_norm_in:
        inv_rms = lax.rsqrt(ssq_in_ref[:, :1] * (1.0 / lhs.shape[1]) + NORM_EPS)
        accs = [a * inv_rms for a in accs]
    out = epilogue(accs, res)
    o_ref[...] = out.astype(o_ref.dtype)
    if has_norm_out:
        xg_ref, ssq_ref = next(refs), next(refs)
        xg_ref[...] = (out * gain_ref[...]).astype(xg_ref.dtype)
        part = jnp.broadcast_to(jnp.sum(out * out, axis=1, keepdims=True), ssq_ref.shape)
        first = pl.program_id(1) == 0

        @pl.when(first)
        def _():
            ssq_ref[...] = part

        @pl.when(jnp.logical_not(first))
        def _():
            ssq_ref[...] += part


def matmul(lhs, rhs, rhs_col_offsets, n_out, epilogue, out_dtype, res=None, rhs_lead=(), norm_in=None,
           norm_out_gain=None, lhs_in_segment_rows=False, tm=1024, tn=512, name="matmul"):
    m, k = lhs.shape
    tn = min(tn, n_out)
    tm = min(tm, m // SCAN_SEGS if lhs_in_segment_rows else m)
    n_rhs = len(rhs_col_offsets)
    lhs_mode = pl.Buffered(1) if rhs.dtype == F32 else None
    if lhs_in_segment_rows:
        per_seg = m // SCAN_SEGS // tm
        lhs = lhs.reshape(m // SCAN_SEGS, SCAN_SEGS * k)
        in_specs = [pl.BlockSpec((tm, k), lambda i, j: _segment_block(i, per_seg))]
    else:
        in_specs = [pl.BlockSpec((tm, k), lambda i, j: (i, 0), pipeline_mode=lhs_mode)]
    lead_block = (None,) * len(rhs_lead)
    for off in rhs_col_offsets:
        in_specs.append(pl.BlockSpec(lead_block + (k, tn),
                                     functools.partial(lambda i, j, ob: rhs_lead + (0, j + ob), ob=off // tn)))
    args = [lhs] + [rhs] * n_rhs
    tile_spec = pl.BlockSpec((tm, tn), lambda i, j: (i, j))
    row_spec = pl.BlockSpec((tm, V7X_LANES), lambda i, j: (i, 0))
    if res is not None:
        in_specs.append(tile_spec)
        args.append(res)
    if norm_in is not None:
        in_specs.append(row_spec)
        args.append(norm_in)
    out_specs, out_shape = tile_spec, jax.ShapeDtypeStruct((m, n_out), out_dtype)
    if norm_out_gain is not None:
        in_specs.append(pl.BlockSpec((1, tn), lambda i, j: (0, j)))
        args.append(norm_out_gain.reshape(1, n_out).astype(F32))
        out_specs = [tile_spec, tile_spec, row_spec]
        out_shape = [out_shape, jax.ShapeDtypeStruct((m, n_out), BF16), jax.ShapeDtypeStruct((m, V7X_LANES), F32)]
    return pl.pallas_call(
        functools.partial(_mm_kernel, n_rhs=n_rhs, has_res=res is not None, has_norm_in=norm_in is not None,
                          has_norm_out=norm_out_gain is not None, epilogue=epilogue),
        grid=(m // tm, n_out // tn),
        in_specs=in_specs,
        out_specs=out_specs,
        out_shape=out_shape,
        compiler_params=_params("parallel", "arbitrary"),
        name=name,
    )(*args)


def _ep_swiglu(accs, res):
    a, b = accs
    return (a * jax.nn.sigmoid(a)) * b


def _ep_half_residual(accs, res):
    return res + 0.5 * accs[0]


def _ep_residual(accs, res):
    return res + accs[0]


def _ep_glu_residual(accs, res):
    val, gate = accs
    return res + val * jax.nn.sigmoid(gate)


def _ep_scale_q(accs, res):
    return accs[0] * (HEAD_DIM ** -0.5 * math.log2(math.e))


def _ep_identity(accs, res):
    return accs[0]


def _ep_sigmoid(accs, res):
    return jax.nn.sigmoid(accs[0])


def _with_norm(result, next_gain):
    return (result[0], (result[1], result[2])) if next_gain is not None else (result, None)


def ffn(x, g, w_in, w_out, lead=(), pre=None, next_gain=None):
    if pre is None:
        act = matmul(rmsnorm(x, g, BF16), w_in, (0, FFN_DIM), FFN_DIM, _ep_swiglu, BF16, rhs_lead=lead, name="ffn_in")
    else:
        act = matmul(pre[0], w_in, (0, FFN_DIM), FFN_DIM, _ep_swiglu, BF16, rhs_lead=lead, norm_in=pre[1],
                     name="ffn_in")
    out = matmul(act, w_out, (0,), D_MODEL, _ep_half_residual, F32, res=x, rhs_lead=lead, norm_out_gain=next_gain,
                 name="ffn_out")
    return _with_norm(out, next_gain)


def _s5_discretise(lam_re, lam_im, log_step, b_re, b_im, seg_len):
    dt = jnp.exp(log_step.astype(F32))[:, None]
    lr, li = lam_re.astype(F32), lam_im.astype(F32)
    mag = jnp.exp(lr * dt)
    ang = li * dt
    ab_re, ab_im = mag * jnp.cos(ang), mag * jnp.sin(ang)
    den = lr * lr + li * li
    nr, ni = ab_re - 1.0, ab_im
    coef_re = (nr * lr + ni * li) / den
    coef_im = (ni * lr - nr * li) / den
    br, bi = b_re.astype(F32), b_im.astype(F32)
    bb_re = coef_re[..., None] * br - coef_im[..., None] * bi
    bb_im = coef_re[..., None] * bi + coef_im[..., None] * br
    mag_n = jnp.exp(seg_len * (lr * dt))
    ang_n = seg_len * ang
    return ab_re, ab_im, bb_re, bb_im, mag_n * jnp.cos(ang_n), mag_n * jnp.sin(ang_n)


def _s5_layout(ab_re, ab_im, bb_re, bb_im, c_re, c_im, an_re, an_im):
    eye = jnp.eye(SCAN_SUPER, dtype=F32)

    def in_proj(bb):
        w = bb.reshape(N_SUPER, SCAN_SUPER, SSM_STATE, SSM_GROUP).transpose(0, 1, 3, 2)
        w = jnp.einsum('sghp,gk->sghkp', w, eye)
        return w.reshape(N_SUPER, SCAN_CH, SCAN_COLS)

    def out_proj(c):
        w = c.reshape(N_SUPER, SCAN_SUPER, SSM_GROUP, SSM_STATE).transpose(0, 1, 3, 2)
        w = jnp.einsum('sgph,gk->sgpkh', w, eye)
        return w.reshape(N_SUPER, SCAN_COLS, SCAN_CH)

    w_b = jnp.concatenate([in_proj(bb_re), in_proj(bb_im)], axis=2).astype(BF16)
    w_c = jnp.concatenate([out_proj(c_re.astype(F32)), -out_proj(c_im.astype(F32))], axis=1).astype(BF16)

    def lanes(a):
        a = a.reshape(N_SUPER, 1, SCAN_COLS)
        return jnp.broadcast_to(a, (N_SUPER, SCAN_SEGS, SCAN_COLS))

    return w_b, w_c, lanes(ab_re), lanes(ab_im), an_re.reshape(N_SUPER, SCAN_COLS), an_im.reshape(N_SUPER, SCAN_COLS)


def _s5_chunk_pipeline(u_ref, wb_ref, wc_ref, d_ref, g_ref, bu_scr, h_scr, a_re, a_im, h_re, h_im, tb):
    n_chunks = tb // SCAN_CHUNK
    chunk_rows = SCAN_CHUNK * SCAN_SEGS

    def in_proj(k):
        rows = slice(k * chunk_rows, (k + 1) * chunk_rows)
        bu_scr[rows, :] = _dot(u_ref[rows, :].astype(BF16), wb_ref[0])

    def out_proj(k):
        rows = slice(k * chunk_rows, (k + 1) * chunk_rows)
        y = _dot(h_scr[rows, :].astype(BF16), wc_ref[0]) + d_ref[...] * u_ref[rows, :]
        g_ref[rows, :] = _gelu(y).astype(g_ref.dtype)

    in_proj(0)
    for k in range(n_chunks):
        if k + 1 < n_chunks:
            in_proj(k + 1)
        for t in range(k * SCAN_CHUNK, (k + 1) * SCAN_CHUNK):
            rows = slice(t * SCAN_SEGS, (t + 1) * SCAN_SEGS)
            bu = bu_scr[rows, :]
            h_re, h_im = ((a_re * h_re - a_im * h_im) + bu[:, :SCAN_COLS],
                          (a_re * h_im + a_im * h_re) + bu[:, SCAN_COLS:])
            if h_scr is not None:
                h_scr[rows, :SCAN_COLS] = h_re
                h_scr[rows, SCAN_COLS:] = h_im
        if h_scr is not None and k > 0:
            out_proj(k - 1)
    if h_scr is not None:
        out_proj(n_chunks - 1)
    return h_re, h_im


def _s5_state_kernel(u_ref, wb_ref, are_ref, aim_ref, hend_ref, bu_scr, st_scr, *, tb):
    tblk = pl.program_id(1)

    @pl.when(tblk == 0)
    def _():
        st_scr[...] = jnp.zeros_like(st_scr)

    st = st_scr[...]
    hr, hi = _s5_chunk_pipeline(u_ref, wb_ref, None, None, None, bu_scr, None, are_ref[0], aim_ref[0],
                                st[:, :SCAN_COLS], st[:, SCAN_COLS:], tb)
    st_scr[:, :SCAN_COLS] = hr
    st_scr[:, SCAN_COLS:] = hi

    @pl.when(tblk == pl.num_programs(1) - 1)
    def _():
        hend_ref[0] = st_scr[...]


def _s5_carry_kernel(hend_ref, an_re_ref, an_im_ref, s_ref):
    ar, ai = an_re_ref[...], an_im_ref[...]
    sr = jnp.zeros_like(ar)
    si = jnp.zeros_like(ar)
    for c in range(SCAN_SEGS):
        s_ref[c, :, :SCAN_COLS] = sr
        s_ref[c, :, SCAN_COLS:] = si
        hr = hend_ref[c, :, :SCAN_COLS]
        hi = hend_ref[c, :, SCAN_COLS:]
        sr, si = (ar * sr - ai * si) + hr, (ar * si + ai * sr) + hi


def _s5_output_kernel(u_ref, wb_ref, wc_ref, are_ref, aim_ref, s0_ref, d_ref, g_ref, bu_scr, h_scr, st_scr, *, tb):
    tblk = pl.program_id(1)

    @pl.when(tblk == 0)
    def _():
        st_scr[...] = s0_ref[0]

    st = st_scr[...]
    hr, hi = _s5_chunk_pipeline(u_ref, wb_ref, wc_ref, d_ref, g_ref, bu_scr, h_scr, are_ref[0], aim_ref[0],
                                st[:, :SCAN_COLS], st[:, SCAN_COLS:], tb)
    st_scr[:, :SCAN_COLS] = hr
    st_scr[:, SCAN_COLS:] = hi


def s5_mixer(x, g_norm, lam_re, lam_im, log_step, b_re, b_im, c_re, c_im, d_skip, w_glu, next_gain=None, tb=128):
    seq = x.shape[0]
    seg_len = seq // SCAN_SEGS
    tb = min(tb, seg_len)
    assert tb % SCAN_CHUNK == 0
    rows = tb * SCAN_SEGS
    n_tblk = seg_len // tb
    u = rmsnorm(x, g_norm, F32, to_segment_rows=True)
    disc = _s5_discretise(lam_re, lam_im, log_step, b_re, b_im, seg_len)
    w_b, w_c, a_re, a_im, an_re, an_im = _s5_layout(disc[0], disc[1], disc[2], disc[3], c_re, c_im, disc[4], disc[5])

    u_spec = pl.BlockSpec((rows, SCAN_CH), lambda s, t: (t, s))
    wb_spec = pl.BlockSpec((1, SCAN_CH, 2 * SCAN_COLS), lambda s, t: (s, 0, 0))
    a_spec = pl.BlockSpec((1, SCAN_SEGS, SCAN_COLS), lambda s, t: (s, 0, 0))
    st_spec = pl.BlockSpec((1, SCAN_SEGS, 2 * SCAN_COLS), lambda s, t: (s, 0, 0))

    h_end = pl.pallas_call(
        functools.partial(_s5_state_kernel, tb=tb),
        grid=(N_SUPER, n_tblk),
        in_specs=[u_spec, wb_spec, a_spec, a_spec],
        out_specs=st_spec,
        out_shape=jax.ShapeDtypeStruct((N_SUPER, SCAN_SEGS, 2 * SCAN_COLS), F32),
        scratch_shapes=[pltpu.VMEM((rows, 2 * SCAN_COLS), F32), pltpu.VMEM((SCAN_SEGS, 2 * SCAN_COLS), F32)],
        compiler_params=_params("parallel", "arbitrary"),
        name="s5_state",
    )(u, w_b, a_re, a_im)

    s0 = pl.pallas_call(
        _s5_carry_kernel,
        out_shape=jax.ShapeDtypeStruct((SCAN_SEGS, N_SUPER, 2 * SCAN_COLS), F32),
        name="s5_carry",
    )(h_end.transpose(1, 0, 2), an_re, an_im)
    s0 = s0.transpose(1, 0, 2)

    act = pl.pallas_call(
        functools.partial(_s5_output_kernel, tb=tb),
        grid=(N_SUPER, n_tblk),
        in_specs=[u_spec, wb_spec,
                  pl.BlockSpec((1, 2 * SCAN_COLS, SCAN_CH), lambda s, t: (s, 0, 0)),
                  a_spec, a_spec, st_spec,
                  pl.BlockSpec((1, SCAN_CH), lambda s, t: (0, s))],
        out_specs=pl.BlockSpec((rows, SCAN_CH), lambda s, t: (t, s)),
        out_shape=jax.ShapeDtypeStruct((seq, D_MODEL), BF16),
        scratch_shapes=[pltpu.VMEM((rows, 2 * SCAN_COLS), F32), pltpu.VMEM((rows, 2 * SCAN_COLS), F32),
                        pltpu.VMEM((SCAN_SEGS, 2 * SCAN_COLS), F32)],
        compiler_params=_params("parallel", "arbitrary"),
        name="s5_output",
    )(u, w_b, w_c, a_re, a_im, s0, d_skip.reshape(1, D_MODEL).astype(F32))

    out = matmul(act, w_glu, (0, D_MODEL), D_MODEL, _ep_glu_residual, F32, res=x, norm_out_gain=next_gain,
                 lhs_in_segment_rows=True, name="s5_glu")
    return _with_norm(out, next_gain)


def _compress_kernel(x_ref, pe_ref, w1_ref, w2_ref, o_ref):
    x = x_ref[0]
    n_half = x.shape[0]
    half = (CMP_BLOCK // 2) * HEAD_DIM
    pe = pe_ref[0]
    xa = (x + pe[0:1, :]).astype(BF16)
    xb = (x + pe[1:2, :]).astype(BF16)
    top = _dot(xa, w1_ref[0, :half, :])
    bot = _dot(xb, w1_ref[0, half:, :])
    bot_next = pltpu.roll(bot, n_half - 1, 0)
    hid = _gelu(top + bot_next)
    out = _dot(hid.astype(BF16), w2_ref[0])
    row = lax.broadcasted_iota(jnp.int32, out.shape, 0)
    o_ref[0] = jnp.where(row < n_half - 1, out, 0.0).astype(o_ref.dtype)


def compress(kv_heads, pe, w1, w2):
    n_hd, n_half, width = kv_heads.shape
    return pl.pallas_call(
        _compress_kernel,
        grid=(n_hd,),
        in_specs=[pl.BlockSpec((1, n_half, width), lambda i: (i, 0, 0)),
                  pl.BlockSpec((1, 2, width), lambda i: (i // N_KV_HEADS, 0, 0)),
                  pl.BlockSpec((1, 2 * width, CMP_HIDDEN), lambda i: (i // N_KV_HEADS, 0, 0)),
                  pl.BlockSpec((1, CMP_HIDDEN, HEAD_DIM), lambda i: (i // N_KV_HEADS, 0, 0))],
        out_specs=pl.BlockSpec((1, n_half, HEAD_DIM), lambda i: (i, 0, 0)),
        out_shape=jax.ShapeDtypeStruct((n_hd, n_half, HEAD_DIM), BF16),
        compiler_params=_params("parallel"),
        name="nsa_compress",
    )(kv_heads, pe, w1, w2)


def _split3_dot(p, w):
    p1 = p.astype(BF16)
    r1 = p - p1.astype(F32)
    p2 = r1.astype(BF16)
    p3 = (r1 - p2.astype(F32)).astype(BF16)
    return _dot(p1, w) + _dot(p2, w) + _dot(p3, w)


def _head_rows(q_blk, g):
    base = g * HEADS_PER_KV * HEAD_DIM
    return jnp.concatenate(
        [q_blk[:, base + n * HEAD_DIM:base + (n + 1) * HEAD_DIM] for n in range(HEADS_PER_KV)], axis=0)


def _select_kernel(q_ref, kc_ref, ov_ref, oc_ref, bias_ref, imp_scr, *, n_sb, n_sbp, cmp_bucket):
    qb = pl.program_id(0)
    n_cmp = kc_ref.shape[1]
    t_row = qb * Q_TILE + (lax.broadcasted_iota(jnp.int32, (CMP_ROWS, 1), 0) & (Q_TILE - 1))
    has_key = (t_row >= CMP_BLOCK - 1).astype(F32)

    def compressed_branch(n_cols):
        q_blk = q_ref[...]
        c_end = lax.broadcasted_iota(jnp.int32, (1, n_cols), 1) * CMP_STRIDE + (CMP_BLOCK - 1)
        visible = c_end <= t_row
        heads = CMP_ROWS // Q_TILE
        for g in range(N_KV_HEADS):
            p_sum = None
            for n0 in range(0, HEADS_PER_KV, heads):
                col0 = (g * HEADS_PER_KV + n0) * HEAD_DIM
                qr = jnp.concatenate([q_blk[:, col0 + n * HEAD_DIM:col0 + (n + 1) * HEAD_DIM] for n in range(heads)],
                                     axis=0)
                s = jnp.where(visible, _dot_nt(qr, kc_ref[g, :n_cols, :]), NEG_INF)
                e = jnp.exp2(s - jnp.max(s, axis=1, keepdims=True))
                p_c = e * (has_key / jnp.sum(e, axis=1, keepdims=True))
                o_c = _dot(p_c.astype(BF16), kc_ref[N_KV_HEADS + g, :n_cols, :])
                for n in range(heads):
                    oc_ref[:, col0 + n * HEAD_DIM:col0 + (n + 1) * HEAD_DIM] = o_c[n * Q_TILE:(n + 1) * Q_TILE]
                    part = p_c[n * Q_TILE:(n + 1) * Q_TILE]
                    p_sum = part if p_sum is None else p_sum + part
            imp_scr[g * Q_TILE:(g + 1) * Q_TILE, :] = _split3_dot(p_sum, ov_ref[:n_cols, :])

    last_visible = (Q_TILE // CMP_STRIDE) * qb + (Q_TILE - CMP_BLOCK) // CMP_STRIDE
    for k in range(n_cmp // cmp_bucket):
        pl.when(last_visible // cmp_bucket == k)(functools.partial(compressed_branch, (k + 1) * cmp_bucket))

    rows = N_KV_HEADS * Q_TILE
    t_q = qb * Q_TILE + (lax.broadcasted_iota(jnp.int32, (rows, 1), 0) & (Q_TILE - 1))
    blk = t_q >> int(math.log2(SEL_BLOCK))

    def pick_blocks(n_j):
        imp = imp_scr[:, :n_j]
        j_i = lax.broadcasted_iota(jnp.int32, (1, n_j), 1)
        forced = jnp.where(j_i == 0, 1.0, jnp.where(j_i == blk, 1.0, jnp.where(j_i == blk - 1, 1.0, 0.0)))
        work = jnp.where(forced > 0.0, -3e38, jnp.where(j_i <= blk, imp, -1.0))
        picked = forced
        for _ in range(min(N_SEL, n_sb) - 3):
            hit = j_i == jnp.argmax(work, axis=1, keepdims=True).astype(jnp.int32)
            picked = jnp.where(hit, 1.0, picked)
            work = jnp.where(hit, -3e38, work)
        bias = jnp.where(picked > 0.0, jnp.where(j_i < 2 * qb, 0.0, NEG_INF), NEG_INF).astype(BF16)
        if n_sbp > n_j:
            bias = jnp.concatenate([bias, jnp.full((rows, n_sbp - n_j), NEG_INF, BF16)], axis=1)
        for g in range(N_KV_HEADS):
            bias_ref[:, g * n_sbp:(g + 1) * n_sbp] = bias[g * Q_TILE:(g + 1) * Q_TILE]

    n_causal = (Q_TILE // SEL_BLOCK) * (qb + 1)
    if n_sb > BIAS_BLOCKS:
        pl.when(n_causal <= BIAS_BLOCKS)(functools.partial(pick_blocks, BIAS_BLOCKS))
        pl.when(n_causal > BIAS_BLOCKS)(functools.partial(pick_blocks, n_sb))
    else:
        pick_blocks(n_sb)


def _attend_kernel(q_ref, gate_ref, oc_ref, bias_ref, ks_ref, vs_ref, kw_ref, vw_ref, e_ref, o_ref,
                   qa_scr, acc_scr, m_scr, s_scr, ow_scr, *, n_bias, tiles_per_bias):
    qb = pl.program_id(1)
    qr = _head_rows(q_ref[...], 0)
    t_rel = lax.broadcasted_iota(jnp.int32, (Q_ROWS, 1), 0) & (Q_TILE - 1)
    t_row = qb * Q_TILE + t_rel
    ones_v = jnp.ones((max(WINDOW + Q_TILE, SEL_TILE), HEAD_DIM), BF16)

    bias = bias_ref[...]
    for a in range(n_bias):
        qa_scr[a, :, :HEAD_DIM] = qr
        for n in range(HEADS_PER_KV):
            qa_scr[a, n * Q_TILE:(n + 1) * Q_TILE, HEAD_DIM:] = bias[:, a * BIAS_BLOCKS:(a + 1) * BIAS_BLOCKS]

    k0 = pl.multiple_of(qb * Q_TILE, Q_TILE)
    s = _dot_nt(qr, ks_ref[pl.ds(k0, Q_TILE), :])
    k_rel = lax.broadcasted_iota(jnp.int32, (1, Q_TILE), 1)
    s = jnp.where(k_rel <= t_rel, s, NEG_INF)
    m0 = jnp.max(s, axis=1, keepdims=True)
    p = jnp.exp2(s - m0).astype(BF16)
    acc_scr[...] = _dot(p, jnp.concatenate([vs_ref[pl.ds(k0, Q_TILE), :], ones_v[:Q_TILE]], axis=1))
    m_scr[...] = jnp.broadcast_to(m0, (Q_ROWS, V7X_LANES))

    band = WINDOW + Q_TILE
    b0 = pl.multiple_of(jnp.maximum(qb * Q_TILE - WINDOW, 0), Q_TILE)
    k_band = kw_ref[pl.ds(b0, band), :]
    v_band = jnp.concatenate([vw_ref[pl.ds(b0, band), :], ones_v[:band]], axis=1)
    k_pos = b0 + lax.broadcasted_iota(jnp.int32, (1, band), 1)
    for c in range(Q_ROWS // WIN_ROWS):
        rows = slice(c * WIN_ROWS, (c + 1) * WIN_ROWS)
        t_c = t_row[rows]
        s = _dot_nt(qr[rows], k_band)
        s = jnp.where(k_pos <= t_c, jnp.where(k_pos > t_c - WINDOW, s, NEG_INF), NEG_INF)
        p = jnp.exp2(s - jnp.max(s, axis=1, keepdims=True)).astype(BF16)
        acc_w = _dot(p, v_band)
        ow_scr[rows, :] = acc_w[:, :HEAD_DIM] / acc_w[:, HEAD_DIM:]

    n_tiles = (qb * Q_TILE + SEL_TILE - 1) // SEL_TILE
    chunks = [pl.ds(c * SEL_ROWS, SEL_ROWS) for c in range(Q_ROWS // SEL_ROWS)]

    def keys_of(i):
        r0 = pl.multiple_of(i * SEL_TILE, SEL_TILE)
        a = i // tiles_per_bias
        e0 = pl.multiple_of((i - a * tiles_per_bias) * SEL_TILE, SEL_TILE)
        return a, jnp.concatenate([ks_ref[pl.ds(r0, SEL_TILE), :], e_ref[pl.ds(e0, SEL_TILE), :]], axis=1)

    a_first, k_first = keys_of(0)
    for rows in chunks:
        s_scr[rows, :] = _dot_nt(qa_scr[a_first, rows, :], k_first)

    def sel_step(i, has_next):
        r0 = pl.multiple_of(i * SEL_TILE, SEL_TILE)
        v_aug = jnp.concatenate([vs_ref[pl.ds(r0, SEL_TILE), :], ones_v[:SEL_TILE]], axis=1)
        if has_next:
            a_next, k_next = keys_of(i + 1)
        for rows in chunks:
            st = s_scr[rows, :]
            m_old = m_scr[rows, :]
            m_new = jnp.maximum(m_old, jnp.max(st, axis=1, keepdims=True))
            pt = jnp.exp2(st - jnp.concatenate([m_new] * (SEL_TILE // V7X_LANES), axis=1)).astype(BF16)
            alpha = jnp.exp2(m_old - m_new)
            acc_scr[rows, :] = jnp.concatenate([alpha, alpha], axis=1) * acc_scr[rows, :] + _dot(pt, v_aug)
            m_scr[rows, :] = m_new
            if has_next:
                s_scr[rows, :] = _dot_nt(qa_scr[a_next, rows, :], k_next)

    n_piped = jnp.maximum(n_tiles - 1, 0)

    def sel_pair(ii, carry):
        sel_step(2 * ii, True)
        sel_step(2 * ii + 1, True)
        return carry

    lax.fori_loop(0, n_piped // 2, sel_pair, 0)

    @pl.when(n_piped % 2 == 1)
    def _():
        sel_step(n_piped - 1, True)

    @pl.when(n_tiles > 0)
    def _():
        sel_step(n_tiles - 1, False)

    gates = gate_ref[0]
    for n in range(HEADS_PER_KV):
        rows = pl.ds(n * Q_TILE, Q_TILE)
        cols = slice(n * HEAD_DIM, (n + 1) * HEAD_DIM)
        o_s = acc_scr[rows, :HEAD_DIM] / acc_scr[rows, HEAD_DIM:]
        o_n = (gates[0, :, n:n + 1] * oc_ref[:, cols] + gates[1, :, n:n + 1] * o_s
               + gates[2, :, n:n + 1] * ow_scr[rows, :])
        o_ref[:, cols] = o_n.astype(o_ref.dtype)


def nsa_attention(q, gates, kv_cmp, kv, overlap, onehot):
    seq = q.shape[0]
    n_cmp = kv_cmp.shape[1]
    n_sb = seq // SEL_BLOCK
    n_bias = pl.cdiv(n_sb, BIAS_BLOCKS)
    n_sbp = n_bias * BIAS_BLOCKS
    e_rows = onehot.shape[0]
    group_w = HEADS_PER_KV * HEAD_DIM

    o_cmp, bias = pl.pallas_call(
        functools.partial(_select_kernel, n_sb=n_sb, n_sbp=n_sbp, cmp_bucket=min(CMP_BUCKET, n_cmp)),
        grid=(seq // Q_TILE,),
        in_specs=[pl.BlockSpec((Q_TILE, Q_WIDTH), lambda i: (i, 0)),
                  pl.BlockSpec((2 * N_KV_HEADS, n_cmp, HEAD_DIM), lambda i: (0, 0, 0)),
                  pl.BlockSpec((n_cmp, n_sb), lambda i: (0, 0))],
        out_specs=[pl.BlockSpec((Q_TILE, Q_WIDTH), lambda i: (i, 0)),
                   pl.BlockSpec((Q_TILE, N_KV_HEADS * n_sbp), lambda i: (i, 0))],
        out_shape=[jax.ShapeDtypeStruct((seq, Q_WIDTH), F32),
                   jax.ShapeDtypeStruct((seq, N_KV_HEADS * n_sbp), BF16)],
        scratch_shapes=[pltpu.VMEM((N_KV_HEADS * Q_TILE, n_sb), F32)],
        compiler_params=_params("parallel"),
        name="nsa_select",
    )(q, kv_cmp, overlap)

    resident = pl.Buffered(1)
    kv_spec = lambda col: pl.BlockSpec((seq, HEAD_DIM), functools.partial(lambda g, i, c: (0, c + g), c=col),
                                       pipeline_mode=resident)
    return pl.pallas_call(
        functools.partial(_attend_kernel, n_bias=n_bias, tiles_per_bias=e_rows // SEL_TILE),
        grid=(N_KV_HEADS, seq // Q_TILE),
        in_specs=[pl.BlockSpec((Q_TILE, group_w), lambda g, i: (i, g)),
                  pl.BlockSpec((1, N_BRANCH, Q_TILE, HEADS_PER_KV), lambda g, i: (g, 0, i, 0)),
                  pl.BlockSpec((Q_TILE, group_w), lambda g, i: (i, g)),
                  pl.BlockSpec((Q_TILE, n_sbp), lambda g, i: (i, g)),
                  kv_spec(0), kv_spec(N_KV_HEADS), kv_spec(2 * N_KV_HEADS), kv_spec(3 * N_KV_HEADS),
                  pl.BlockSpec((e_rows, BIAS_BLOCKS), lambda g, i: (0, 0))],
        out_specs=pl.BlockSpec((Q_TILE, group_w), lambda g, i: (i, g)),
        out_shape=jax.ShapeDtypeStruct((seq, Q_WIDTH), BF16),
        scratch_shapes=[pltpu.VMEM((n_bias, Q_ROWS, 2 * HEAD_DIM), BF16),
                        pltpu.VMEM((Q_ROWS, 2 * HEAD_DIM), F32),
                        pltpu.VMEM((Q_ROWS, V7X_LANES), F32),
                        pltpu.VMEM((Q_ROWS, SEL_TILE), F32),
                        pltpu.VMEM((Q_ROWS, HEAD_DIM), F32)],
        compiler_params=_params("parallel", "arbitrary"),
        name="nsa_attend",
    )(q, gates, o_cmp, bias, kv, kv, kv, kv, onehot)


def nsa_mixer(x, g_norm, w_in, w_o, pe, w1, w2, pre=None, next_gain=None):
    seq = x.shape[0]
    h, ssq = (rmsnorm(x, g_norm, BF16), None) if pre is None else pre
    q = matmul(h, w_in, (0,), Q_WIDTH, _ep_scale_q, BF16, norm_in=ssq, name="nsa_q")
    c0 = Q_WIDTH
    kv_c = matmul(h, w_in, (c0,), 2 * KV_WIDTH, _ep_identity, F32, norm_in=ssq, name="nsa_kv_cmp")
    c0 += 2 * KV_WIDTH
    kv = matmul(h, w_in, (c0,), 4 * KV_WIDTH, _ep_identity, BF16, norm_in=ssq, name="nsa_kv")
    c0 += 4 * KV_WIDTH
    n_gate = N_BRANCH * N_HEADS
    gates = matmul(h, w_in, (c0,), V7X_LANES, _ep_sigmoid, F32, norm_in=ssq, name="nsa_gates")
    gates = gates[:, :n_gate].reshape(seq, N_KV_HEADS, HEADS_PER_KV, N_BRANCH).transpose(1, 3, 0, 2)

    n_half = seq // CMP_STRIDE
    kv_heads = kv_c.reshape(seq, 2 * N_KV_HEADS, HEAD_DIM).transpose(1, 0, 2)
    kv_heads = kv_heads.reshape(2 * N_KV_HEADS, n_half, CMP_STRIDE * HEAD_DIM)
    kv_cmp = compress(kv_heads, pe, w1, w2)

    n_sb = seq // SEL_BLOCK
    c_start = jnp.arange(n_half) * CMP_STRIDE
    s_start = jnp.arange(n_sb) * SEL_BLOCK
    overlap = ((c_start[:, None] < s_start[None, :] + SEL_BLOCK)
               & (c_start[:, None] + CMP_BLOCK > s_start[None, :])
               & (jnp.arange(n_half)[:, None] < n_half - 1)).astype(BF16)
    e_rows = min(seq, BIAS_KEYS)
    onehot = ((jnp.arange(e_rows)[:, None] // SEL_BLOCK) == jnp.arange(BIAS_BLOCKS)[None, :]).astype(BF16)

    o = nsa_attention(q, gates, kv_cmp, kv, overlap, onehot)
    out = matmul(o, w_o, (0,), D_MODEL, _ep_residual, F32, res=x, norm_out_gain=next_gain, name="nsa_out")
    return _with_norm(out, next_gain)


def kernel(x, norm_g, final_norm_g, ffn_w_in, ffn_w_out, s5_lambda_re, s5_lambda_im, s5_log_step, s5_b_re, s5_b_im, s5_c_re, s5_c_im, s5_d, s5_w_glu, nsa_w_in, nsa_w_o, nsa_pe_k, nsa_w1_k, nsa_w2_k, nsa_pe_v, nsa_w1_v, nsa_w2_v):
    bsz, seq, d = x.shape
    assert bsz == 1 and d == D_MODEL and seq % (SCAN_SEGS * 128) == 0
    xs = x.reshape(seq, d).astype(F32)
    ffn_w_out = ffn_w_out.astype(BF16)
    s5_w_glu = s5_w_glu.astype(BF16)
    nsa_w_in = nsa_w_in.astype(BF16)
    nsa_w_o = nsa_w_o.astype(BF16)
    half = (CMP_BLOCK // 2) * HEAD_DIM
    pe = jnp.stack([nsa_pe_k, nsa_pe_v], axis=1).astype(F32).reshape(-1, 2, 2, half)
    w1 = jnp.stack([nsa_w1_k, nsa_w1_v], axis=1).astype(BF16)
    w2 = jnp.stack([nsa_w2_k, nsa_w2_v], axis=1).astype(BF16)

    pre = None
    for layer in range(DEPTH):
        j = layer // 2
        is_s5 = layer % 2 == 0
        g_ffn1, g_mix, g_ffn2 = norm_g[layer, 0], norm_g[layer, 1], norm_g[layer, 2]
        xs, pre = ffn(xs, g_ffn1, ffn_w_in, ffn_w_out, (layer, 0), pre, next_gain=None if is_s5 else g_mix)
        if is_s5:
            xs, pre = s5_mixer(xs, g_mix, s5_lambda_re[j], s5_lambda_im[j], s5_log_step[j], s5_b_re[j], s5_b_im[j],
                               s5_c_re[j], s5_c_im[j], s5_d[j], s5_w_glu[j], next_gain=g_ffn2)
        else:
            xs, pre = nsa_mixer(xs, g_mix, nsa_w_in[j], nsa_w_o[j], pe[j], w1[j], w2[j], pre, next_gain=g_ffn2)
        g_next = norm_g[layer + 1, 0] if layer + 1 < DEPTH else None
        xs, pre = ffn(xs, g_ffn2, ffn_w_in, ffn_w_out, (layer, 1), pre, next_gain=g_next)
    return rmsnorm(xs, final_norm_g, x.dtype).reshape(bsz, seq, d)
```
